```python
import math
import jax, jax.numpy as jnp
from jax import lax
import numpy as np

D_MODEL = 1024
BATCH = 4
SEQ = 8192
DEPTH = 1
DEC_BATCH = 128
DEC_SEQ = 1
PAST_LEN = 8192
PAGE_SIZE = 128

HEAD_DIM = 64
H_MOBA = 8
H_FOX = 8
W_MOBA = H_MOBA * HEAD_DIM
W_FOX = H_FOX * HEAD_DIM
MOBA_BLOCK = 256
MOBA_TOPK = 3
MOBA_Q_CHUNK = 32
FOX_Q_BLOCK = 128
D_FF = 2816
ROPE_THETA = 10000.0
NORM_EPS = 1e-6
NEG_INF = -1e30
N_MOD = 9
D_MIX_IN = 3 * W_MOBA + 3 * W_FOX + H_FOX + 2 * D_MODEL
MIX_SPLITS = [W_MOBA, 2 * W_MOBA, 3 * W_MOBA, 3 * W_MOBA + W_FOX, 3 * W_MOBA + 2 * W_FOX,
              3 * W_MOBA + 3 * W_FOX, 3 * W_MOBA + 3 * W_FOX + H_FOX,
              3 * W_MOBA + 3 * W_FOX + H_FOX + D_MODEL]

kernel_name = "hybrid_moba_fox_macaron_adaln_decode_step"


def rms_norm(x, g):
    xf = x.astype(jnp.float32)
    y = xf * lax.rsqrt(jnp.mean(xf * xf, axis=-1, keepdims=True) + NORM_EPS)
    return (y * g.astype(jnp.float32)).astype(x.dtype)


def modulate(x, shift, scale):
    return x * (1 + scale) + shift


def swiglu(h, w_in, w_out):
    a, b = jnp.split(h @ w_in, 2, axis=-1)
    return (jax.nn.silu(a) * b) @ w_out


def rope(x, pos):
    half = HEAD_DIM // 2
    inv = ROPE_THETA ** (-jnp.arange(half, dtype=jnp.float32) / half)
    ang = pos.astype(jnp.float32)[:, None] * inv[None, :]
    cos = jnp.cos(ang)[:, None, :]
    sin = jnp.sin(ang)[:, None, :]
    xf = x.astype(jnp.float32)
    x1, x2 = xf[..., :half], xf[..., half:]
    return jnp.concatenate([x1 * cos - x2 * sin, x2 * cos + x1 * sin], axis=-1).astype(x.dtype)


def to_blocks(k):
    b, l, h, d = k.shape
    nb = -(-l // MOBA_BLOCK)
    k = jnp.pad(k, ((0, 0), (0, nb * MOBA_BLOCK - l), (0, 0), (0, 0)))
    return k.reshape(b, nb, MOBA_BLOCK, h, d).transpose(0, 3, 1, 2, 4)


def moba_attend(q, q_pos, kb, vb, km):
    nb = kb.shape[2]
    own = q_pos // MOBA_BLOCK
    qf = q.astype(jnp.float32)
    gate = jnp.einsum('bqhd,bhnd->bhqn', qf, km)
    past = jnp.arange(nb)[None, :] < own[:, None]
    gate = jnp.where(past, gate, NEG_INF)
    n_sel = min(MOBA_TOPK, nb)
    _, top_idx = lax.top_k(gate, n_sel)
    top_valid = top_idx < own[:, None]
    own_idx = jnp.broadcast_to(own[:, None], top_idx.shape[:-1] + (1,)).astype(top_idx.dtype)
    idx = jnp.concatenate([top_idx, own_idx], axis=-1)
    valid = jnp.concatenate([top_valid, jnp.ones(own_idx.shape, dtype=bool)], axis=-1)
    take = jax.vmap(jax.vmap(lambda t, i: t[i]))
    kg = take(kb, idx).astype(jnp.float32)
    vg = take(vb, idx).astype(jnp.float32)
    s = jnp.einsum('bqhd,bhqnkd->bhqnk', qf, kg) * (1.0 / math.sqrt(HEAD_DIM))
    key_pos = idx[..., None] * MOBA_BLOCK + jnp.arange(MOBA_BLOCK)
    mask = valid[..., None] & (key_pos <= q_pos[:, None, None])
    s = jnp.where(mask, s, NEG_INF)
    b, h, lq, n, blk = s.shape
    p = jax.nn.softmax(s.reshape(b, h, lq, n * blk), axis=-1).reshape(s.shape)
    o = jnp.einsum('bhqnk,bhqnkd->bqhd', p, vg)
    return o.astype(q.dtype)


def fox_attend(q, q_cum, q_pos, k, v, k_cum):
    qf = q.astype(jnp.float32)
    s = jnp.einsum('bqhd,bkhd->bhqk', qf, k.astype(jnp.float32)) * (1.0 / math.sqrt(HEAD_DIM))
    s = s + (jnp.transpose(q_cum, (0, 2, 1))[..., None] - jnp.transpose(k_cum, (0, 2, 1))[:, :, None, :])
    lk = k.shape[1]
    mask = jnp.arange(lk)[None, :] <= q_pos[:, None]
    s = jnp.where(mask, s, NEG_INF)
    p = jax.nn.softmax(s, axis=-1)
    o = jnp.einsum('bhqk,bkhd->bqhd', p, v.astype(jnp.float32))
    return o.astype(q.dtype)


def decoder_layer(x, c, pos, attend, w_ada, b_ada, g_ff1, w_ff1_in, w_ff1_out, g_mix, w_mix,
                  b_forget, w_o_moba, w_o_fox, w_out, g_ff2, w_ff2_in, w_ff2_out):
    bsz, L, _ = x.shape
    mod = (jax.nn.silu(c) @ w_ada + b_ada)[:, None, :]
    sh1, sc1, ga1, sh2, sc2, ga2, sh3, sc3, ga3 = jnp.split(mod, N_MOD, axis=-1)
    h = x + 0.5 * ga1 * swiglu(modulate(rms_norm(x, g_ff1), sh1, sc1), w_ff1_in, w_ff1_out)
    n = modulate(rms_norm(h, g_mix), sh2, sc2)
    z = n @ w_mix
    q_a, k_a, v_a, q_b, k_b, v_b, f_logit, ga_logit, gb_logit = jnp.split(z, MIX_SPLITS, axis=-1)
    q_a = rope(q_a.reshape(bsz, L, H_MOBA, HEAD_DIM), pos)
    k_a = rope(k_a.reshape(bsz, L, H_MOBA, HEAD_DIM), pos)
    v_a = v_a.reshape(bsz, L, H_MOBA, HEAD_DIM)
    q_b = q_b.reshape(bsz, L, H_FOX, HEAD_DIM)
    k_b = k_b.reshape(bsz, L, H_FOX, HEAD_DIM)
    v_b = v_b.reshape(bsz, L, H_FOX, HEAD_DIM)
    logf = jax.nn.log_sigmoid((f_logit + b_forget).astype(jnp.float32))
    o_a, o_b = attend(q_a, k_a, v_a, q_b, k_b, v_b, logf)
    y_a = o_a.reshape(bsz, L, W_MOBA) @ w_o_moba
    y_b = o_b.reshape(bsz, L, W_FOX) @ w_o_fox
    merged = jax.nn.sigmoid(ga_logit) * y_a + jax.nn.sigmoid(gb_logit) * y_b
    h = h + ga2 * (merged @ w_out)
    h = h + 0.5 * ga3 * swiglu(modulate(rms_norm(h, g_ff2), sh3, sc3), w_ff2_in, w_ff2_out)
    return h, (k_a, v_a, k_b, v_b, logf)


def prompt_attend(q_a, k_a, v_a, q_b, k_b, v_b, logf):
    bsz, L = q_a.shape[:2]
    kbk = to_blocks(k_a)
    vbk = to_blocks(v_a)
    km = jnp.mean(kbk.astype(jnp.float32), axis=3)
    nc = L // MOBA_Q_CHUNK
    qc = q_a.reshape(bsz, nc, MOBA_Q_CHUNK, H_MOBA, HEAD_DIM).swapaxes(0, 1)
    st = jnp.arange(nc, dtype=jnp.int32) * MOBA_Q_CHUNK
    o_a = lax.map(lambda a: moba_attend(a[0], a[1] + jnp.arange(MOBA_Q_CHUNK, dtype=jnp.int32), kbk, vbk, km), (qc, st))
    o_a = o_a.swapaxes(0, 1).reshape(bsz, L, H_MOBA, HEAD_DIM)
    cum = jnp.cumsum(logf, axis=1)
    nq = L // FOX_Q_BLOCK
    qb = q_b.reshape(bsz, nq, FOX_Q_BLOCK, H_FOX, HEAD_DIM).swapaxes(0, 1)
    cb = cum.reshape(bsz, nq, FOX_Q_BLOCK, H_FOX).swapaxes(0, 1)
    stb = jnp.arange(nq, dtype=jnp.int32) * FOX_Q_BLOCK
    o_b = lax.map(lambda a: fox_attend(a[0], a[1], a[2] + jnp.arange(FOX_Q_BLOCK, dtype=jnp.int32), k_b, v_b, cum), (qb, cb, stb))
    o_b = o_b.swapaxes(0, 1).reshape(bsz, L, H_FOX, HEAD_DIM)
    return o_a, o_b


def make_sample_attend(layer, cache_moba_k, cache_moba_v, cache_fox_k, cache_fox_v, cache_fox_logf, page_table, pos_s):
    past_len = page_table.shape[1] * PAGE_SIZE

    def one(args):
        qa, ka, va, qb, kb, vb, lf, pages = args
        ka_all = jnp.concatenate([cache_moba_k[layer, pages].reshape(past_len, H_MOBA, HEAD_DIM), ka], axis=0)[None]
        va_all = jnp.concatenate([cache_moba_v[layer, pages].reshape(past_len, H_MOBA, HEAD_DIM), va], axis=0)[None]
        kbk = to_blocks(ka_all)
        vbk = to_blocks(va_all)
        km = jnp.mean(kbk.astype(jnp.float32), axis=3)
        oa = moba_attend(qa[None], pos_s, kbk, vbk, km)[0]
        kb_all = jnp.concatenate([cache_fox_k[layer, pages].reshape(past_len, H_FOX, HEAD_DIM), kb], axis=0)[None]
        vb_all = jnp.concatenate([cache_fox_v[layer, pages].reshape(past_len, H_FOX, HEAD_DIM), vb], axis=0)[None]
        lf_all = jnp.concatenate([cache_fox_logf[layer, pages].reshape(past_len, H_FOX).astype(jnp.float32), lf], axis=0)
        cum = jnp.cumsum(lf_all, axis=0)
        ob = fox_attend(qb[None], cum[past_len:][None], pos_s, kb_all, vb_all, cum[None])[0]
        return oa, ob

    def attend(q_a, k_a, v_a, q_b, k_b, v_b, logf):
        return lax.map(one, (q_a, k_a, v_a, q_b, k_b, v_b, logf, page_table))

    return attend


def setup_inputs(seed: int = 0) -> dict:
    key = jax.random.key(seed)
    ks = jax.random.split(key, 32)
    n_pages = PAST_LEN // PAGE_SIZE
    n_used = DEC_BATCH * n_pages
    n_pool = (n_used * 5 + 3) // 4
    f32 = jnp.float32

    def nrm(k, shape, scale):
        return jax.random.normal(k, shape, dtype=f32) * scale

    page_table = jax.random.permutation(ks[0], n_pool)[:n_used].reshape(DEC_BATCH, n_pages).astype(jnp.int32)
    return {
        "x_prompt": nrm(ks[1], (BATCH, SEQ, D_MODEL), 1.0),
        "x_sample": nrm(ks[2], (DEC_BATCH, DEC_SEQ, D_MODEL), 1.0),
        "cache_moba_k": nrm(ks[3], (DEPTH, n_pool, PAGE_SIZE, H_MOBA, HEAD_DIM), 1.0),
        "cache_moba_v": nrm(ks[4], (DEPTH, n_pool, PAGE_SIZE, H_MOBA, HEAD_DIM), 1.0),
        "cache_fox_k": nrm(ks[5], (DEPTH, n_pool, PAGE_SIZE, H_FOX, HEAD_DIM), 1.0),
        "cache_fox_v": nrm(ks[6], (DEPTH, n_pool, PAGE_SIZE, H_FOX, HEAD_DIM), 1.0),
        "cache_fox_logf": jax.nn.log_sigmoid(3.0 + nrm(ks[7], (DEPTH, n_pool, PAGE_SIZE, H_FOX), 0.5)),
        "page_table": page_table,
        "c_prompt": nrm(ks[8], (BATCH, D_MODEL), 1.0),
        "c_sample": nrm(ks[9], (DEC_BATCH, D_MODEL), 1.0),
        "w_ada": nrm(ks[10], (DEPTH, D_MODEL, N_MOD * D_MODEL), 0.5 * D_MODEL ** -0.5),
        "b_ada": nrm(ks[11], (DEPTH, N_MOD * D_MODEL), 0.01),
        "g_ff1": 1.0 + nrm(ks[12], (DEPTH, D_MODEL), 0.01),
        "w_ff1_in": nrm(ks[13], (DEPTH, D_MODEL, 2 * D_FF), D_MODEL ** -0.5),
        "w_ff1_out": nrm(ks[14], (DEPTH, D_FF, D_MODEL), D_FF ** -0.5),
        "g_mix": 1.0 + nrm(ks[15], (DEPTH, D_MODEL), 0.01),
        "w_mix": nrm(ks[16], (DEPTH, D_MODEL, D_MIX_IN), D_MODEL ** -0.5),
        "b_forget": 3.0 + nrm(ks[17], (DEPTH, H_FOX), 0.5),
        "w_o_moba": nrm(ks[18], (DEPTH, W_MOBA, D_MODEL), W_MOBA ** -0.5),
        "w_o_fox": nrm(ks[19], (DEPTH, W_FOX, D_MODEL), W_FOX ** -0.5),
        "w_out": nrm(ks[20], (DEPTH, D_MODEL, D_MODEL), D_MODEL ** -0.5),
        "g_ff2": 1.0 + nrm(ks[21], (DEPTH, D_MODEL), 0.01),
        "w_ff2_in": nrm(ks[22], (DEPTH, D_MODEL, 2 * D_FF), D_MODEL ** -0.5),
        "w_ff2_out": nrm(ks[23], (DEPTH, D_FF, D_MODEL), D_FF ** -0.5),
        "g_final": 1.0 + nrm(ks[24], (D_MODEL,), 0.01),
    }


def reference(x_prompt, x_sample, cache_moba_k, cache_moba_v, cache_fox_k, cache_fox_v, cache_fox_logf,
              page_table, c_prompt, c_sample, w_ada, b_ada, g_ff1, w_ff1_in, w_ff1_out, g_mix, w_mix,
              b_forget, w_o_moba, w_o_fox, w_out, g_ff2, w_ff2_in, w_ff2_out, g_final):
    past_len = page_table.shape[1] * PAGE_SIZE
    pos_p = jnp.arange(x_prompt.shape[1], dtype=jnp.int32)
    pos_s = past_len + jnp.arange(x_sample.shape[1], dtype=jnp.int32)
    h_p = x_prompt
    h_s = x_sample
    st_p = [[], [], [], [], []]
    st_s = [[], [], [], [], []]
    for l in range(DEPTH):
        wl = (w_ada[l], b_ada[l], g_ff1[l], w_ff1_in[l], w_ff1_out[l], g_mix[l], w_mix[l], b_forget[l],
              w_o_moba[l], w_o_fox[l], w_out[l], g_ff2[l], w_ff2_in[l], w_ff2_out[l])
        h_p, new_p = decoder_layer(h_p, c_prompt, pos_p, prompt_attend, *wl)
        sample_attend = make_sample_attend(l, cache_moba_k, cache_moba_v, cache_fox_k, cache_fox_v,
                                           cache_fox_logf, page_table, pos_s)
        h_s, new_s = decoder_layer(h_s, c_sample, pos_s, sample_attend, *wl)
        for i in range(5):
            st_p[i].append(new_p[i])
            st_s[i].append(new_s[i])
    y_prompt = rms_norm(h_p, g_final)
    y_sample = rms_norm(h_s, g_final)
    moba_k_prompt = jnp.stack(st_p[0])
    moba_v_prompt = jnp.stack(st_p[1])
    fox_k_prompt = jnp.stack(st_p[2])
    fox_v_prompt = jnp.stack(st_p[3])
    fox_logf_prompt = jnp.stack(st_p[4])
    moba_k_sample = jnp.stack(st_s[0])
    moba_v_sample = jnp.stack(st_s[1])
    fox_k_sample = jnp.stack(st_s[2])
    fox_v_sample = jnp.stack(st_s[3])
    fox_logf_sample = jnp.stack(st_s[4])
    return (y_prompt, y_sample, moba_k_prompt, moba_v_prompt, fox_k_prompt, fox_v_prompt, fox_logf_prompt,
            moba_k_sample, moba_v_sample, fox_k_sample, fox_v_sample, fox_logf_sample)
```

```python
import functools
import math

import jax
import jax.numpy as jnp
from jax import lax
from jax.experimental import pallas as pl
from jax.experimental.pallas import tpu as pltpu

HEAD_DIM = 64
MOBA_BLOCK = 256
MOBA_TOPK = 3
PAGE_SIZE = 128
ROPE_THETA = 10000.0
NORM_EPS = 1e-6
NEG_INF = -1e30
N_MOD = 9
LANES = 128
HEADS_PER_GROUP = LANES // HEAD_DIM
VMEM_LIMIT = 56 * 1024 * 1024

F32 = jnp.float32
BF16 = jnp.bfloat16
SCALE = 1.0 / math.sqrt(HEAD_DIM)


def _params(*sem):
    return pltpu.CompilerParams(dimension_semantics=sem, vmem_limit_bytes=VMEM_LIMIT)


def _dot(a, b):
    return jnp.dot(a, b, preferred_element_type=F32)


def _dot_nt(a, b):
    return lax.dot_general(a, b, (((1,), (1,)), ((), ())), preferred_element_type=F32)


def _rms(x, g):
    return x * lax.rsqrt(jnp.mean(x * x, axis=-1, keepdims=True) + NORM_EPS) * g


def _silu(x):
    return x * jax.nn.sigmoid(x)


def _mod_kernel(c_ref, w_ref, b_ref, o_ref):
    a = _silu(c_ref[...]).astype(BF16)
    o_ref[...] = _dot(a, w_ref[...].astype(BF16)) + b_ref[...]


def _modulation(c, w_ada, b_ada):
    r, d = c.shape
    return pl.pallas_call(
        _mod_kernel,
        grid=(N_MOD,),
        in_specs=[pl.BlockSpec((r, d), lambda j: (0, 0)),
                  pl.BlockSpec((d, d), lambda j: (0, j)),
                  pl.BlockSpec((1, d), lambda j: (0, j))],
        out_specs=pl.BlockSpec((None, r, d), lambda j: (j, 0, 0)),
        out_shape=jax.ShapeDtypeStruct((N_MOD, r, d), F32),
        compiler_params=_params("arbitrary"),
        name="mod",
    )(c, w_ada, b_ada.reshape(1, -1))


def _mod_spec(mod, k, tm, tiles_per_seq):
    d = mod.shape[-1]
    if mod.ndim == 4:
        return pl.BlockSpec((None, None, 1, d), lambda i, *_: (k, i // tiles_per_seq, 0, 0))
    return pl.BlockSpec((None, tm, d), lambda i, *_: (k, i, 0))


def _ffn_kernel(x_ref, sh_ref, sc_ref, ga_ref, g_ref, wa_ref, wb_ref, wo_ref, *rest, final_norm):
    if final_norm:
        gf_ref, o_ref, n_scr, acc_scr = rest
    else:
        o_ref, n_scr, acc_scr = rest
    c = pl.program_id(1)

    @pl.when(c == 0)
    def _():
        n = _rms(x_ref[...], g_ref[...]) * (1 + sc_ref[...]) + sh_ref[...]
        n_scr[...] = n.astype(BF16)
        acc_scr[...] = jnp.zeros_like(acc_scr)

    n = n_scr[...]
    a = _dot(n, wa_ref[...])
    b = _dot(n, wb_ref[...])
    acc_scr[...] += _dot((_silu(a) * b).astype(BF16), wo_ref[...])

    @pl.when(c == pl.num_programs(1) - 1)
    def _():
        h = x_ref[...] + 0.5 * ga_ref[...] * acc_scr[...]
        if final_norm:
            h = _rms(h, gf_ref[...])
        o_ref[...] = h


def _ffn_tiles(t, d_ff):
    tm = min(t, 512)
    tf = d_ff // 2 if (d_ff // 2) % LANES == 0 else d_ff
    return tm, tf


def _ffn(x, mod, k0, g, w_in, w_out, tiles_per_seq_tokens, g_final=None):
    t, d = x.shape
    d_ff = w_out.shape[0]
    tm, tf = _ffn_tiles(t, d_ff)
    nc = d_ff // tf
    tps = max(tiles_per_seq_tokens // tm, 1)
    in_specs = [pl.BlockSpec((tm, d), lambda i, c: (i, 0)),
                _mod_spec(mod, k0, tm, tps), _mod_spec(mod, k0 + 1, tm, tps), _mod_spec(mod, k0 + 2, tm, tps),
                pl.BlockSpec((1, d), lambda i, c: (0, 0)),
                pl.BlockSpec((d, tf), lambda i, c: (0, c)),
                pl.BlockSpec((d, tf), lambda i, c: (0, c + nc)),
                pl.BlockSpec((tf, d), lambda i, c: (c, 0))]
    args = [x, mod, mod, mod, g.reshape(1, d), w_in, w_in, w_out]
    if g_final is not None:
        in_specs.append(pl.BlockSpec((1, d), lambda i, c: (0, 0)))
        args.append(g_final.reshape(1, d))
    return pl.pallas_call(
        functools.partial(_ffn_kernel, final_norm=g_final is not None),
        grid=(t // tm, nc),
        in_specs=in_specs,
        out_specs=pl.BlockSpec((tm, d), lambda i, c: (i, 0)),
        out_shape=jax.ShapeDtypeStruct((t, d), F32),
        scratch_shapes=[pltpu.VMEM((tm, d), BF16), pltpu.VMEM((tm, d), F32)],
        compiler_params=_params("parallel", "arbitrary"),
        name="ffn_final" if g_final is not None else "ffn",
    )(*args)


def _rope(x, cos, sin_signed):
    w = x.shape[1]
    reps = w // LANES
    cos = jnp.concatenate([cos] * reps, axis=1)
    sin_signed = jnp.concatenate([sin_signed] * reps, axis=1)
    lane = lax.broadcasted_iota(jnp.int32, x.shape, 1)
    first_half = (lane % HEAD_DIM) < (HEAD_DIM // 2)
    swapped = jnp.where(first_half, pltpu.roll(x, w - HEAD_DIM // 2, 1), pltpu.roll(x, HEAD_DIM // 2, 1))
    return x * cos + swapped * sin_signed


def _mix_kernel(h_ref, sh_ref, sc_ref, g_ref, cos_ref, sin_ref, wqkv_ref, wf_ref, bf_ref, wg_ref, *rest,
                w_moba, w_fox, n_fox, seq):
    if seq:
        (ka_ref, va_ref, kb_ref, vb_ref, lf_ref, gl_ref, qa16_ref, ka16_ref, va16_ref, qb16_ref, kb16_ref,
         vb16_ref, cum_ref, cumt_ref, km_ref, carry_scr) = rest
    else:
        ka_ref, va_ref, kb_ref, vb_ref, lf_ref, gl_ref, qa_ref, qb_ref = rest
    n = (_rms(h_ref[...], g_ref[...]) * (1 + sc_ref[...]) + sh_ref[...]).astype(BF16)
    z = _dot(n, wqkv_ref[...])
    o = 0
    qa = _rope(z[:, o:o + w_moba], cos_ref[...], sin_ref[...]); o += w_moba
    ka = _rope(z[:, o:o + w_moba], cos_ref[...], sin_ref[...]); o += w_moba
    va = z[:, o:o + w_moba]; o += w_moba
    qb = z[:, o:o + w_fox]; o += w_fox
    kb = z[:, o:o + w_fox]; o += w_fox
    vb = z[:, o:o + w_fox]
    lf = jax.nn.log_sigmoid(_dot(n, wf_ref[...]) + bf_ref[...])
    ka_ref[...] = ka
    va_ref[...] = va
    kb_ref[...] = kb
    vb_ref[...] = vb
    lf_ref[...] = lf[:, :n_fox]
    gl_ref[...] = _dot(n, wg_ref[...])
    if not seq:
        qa_ref[...] = qa
        qb_ref[...] = qb
        return
    qa16_ref[...] = qa.astype(BF16)
    ka16_ref[...] = ka.astype(BF16)
    va16_ref[...] = va.astype(BF16)
    qb16_ref[...] = qb.astype(BF16)
    kb16_ref[...] = kb.astype(BF16)
    vb16_ref[...] = vb.astype(BF16)
    tm = lf.shape[0]

    @pl.when(pl.program_id(1) == 0)
    def _():
        carry_scr[...] = jnp.zeros_like(carry_scr)

    row = lax.broadcasted_iota(jnp.int32, (tm, tm), 0)
    col = lax.broadcasted_iota(jnp.int32, (tm, tm), 1)
    tri = (row >= col).astype(F32)
    cum = jnp.dot(tri, lf, precision=lax.Precision.HIGHEST, preferred_element_type=F32) + carry_scr[...]
    carry_scr[...] = cum[tm - 1:tm, :]
    cum_ref[...] = cum[:, :n_fox]
    cumt_ref[...] = cum.T[:n_fox, :]
    km_ref[...] = jnp.sum(ka, axis=0, keepdims=True) * (1.0 / MOBA_BLOCK)


def _mix(h, mod, g, cos, sin_signed, wqkv, wf, bf, wg, *, w_moba, w_fox, n_fox, seq_shape):
    t, d = h.shape
    seq = seq_shape is not None
    tm = min(t, MOBA_BLOCK)
    if seq:
        bsz, l = seq_shape
        nt = l // tm
        grid = (bsz, nt)
        tok = lambda b, i: (b * nt + i, 0)
        rope_spec = pl.BlockSpec((tm, LANES), lambda b, i: (i, 0))
        mod_specs = [pl.BlockSpec((None, None, 1, d), functools.partial(lambda b, i, k: (k, b, 0, 0), k=k))
                     for k in (3, 4)]
        sem = ("parallel", "arbitrary")
    else:
        grid = (t // tm,)
        tok = lambda i: (i, 0)
        rope_spec = pl.BlockSpec((1, LANES), lambda i: (0, 0))
        mod_specs = [pl.BlockSpec((None, tm, d), functools.partial(lambda i, k: (k, i, 0), k=k)) for k in (3, 4)]
        sem = ("parallel",)
    const = lambda *_: (0, 0)
    in_specs = [pl.BlockSpec((tm, d), tok), *mod_specs,
                pl.BlockSpec((1, d), const), rope_spec, rope_spec,
                pl.BlockSpec(wqkv.shape, const), pl.BlockSpec(wf.shape, const), pl.BlockSpec(bf.shape, const),
                pl.BlockSpec(wg.shape, const)]
    out_specs = [pl.BlockSpec((tm, w_moba), tok), pl.BlockSpec((tm, w_moba), tok),
                 pl.BlockSpec((tm, w_fox), tok), pl.BlockSpec((tm, w_fox), tok),
                 pl.BlockSpec((tm, n_fox), tok), pl.BlockSpec((tm, wg.shape[1]), tok)]
    out_shape = [jax.ShapeDtypeStruct((t, w_moba), F32), jax.ShapeDtypeStruct((t, w_moba), F32),
                 jax.ShapeDtypeStruct((t, w_fox), F32), jax.ShapeDtypeStruct((t, w_fox), F32),
                 jax.ShapeDtypeStruct((t, n_fox), F32), jax.ShapeDtypeStruct((t, wg.shape[1]), F32)]
    scratch = []
    if seq:
        out_specs += [pl.BlockSpec((tm, w_moba), tok)] * 3 + [pl.BlockSpec((tm, w_fox), tok)] * 3
        out_shape += [jax.ShapeDtypeStruct((t, w_moba), BF16)] * 3 + [jax.ShapeDtypeStruct((t, w_fox), BF16)] * 3
        nbt = tm // MOBA_BLOCK
        out_specs += [pl.BlockSpec((tm, n_fox), tok),
                      pl.BlockSpec((None, n_fox, tm), lambda b, i: (b, 0, i)),
                      pl.BlockSpec((None, None, nbt, w_moba), lambda b, i: (b, i, 0, 0))]
        out_shape += [jax.ShapeDtypeStruct((t, n_fox), F32),
                      jax.ShapeDtypeStruct((bsz, n_fox, l), F32),
                      jax.ShapeDtypeStruct((bsz, nt, nbt, w_moba), F32)]
        scratch = [pltpu.VMEM((1, LANES), F32)]
    else:
        out_specs += [pl.BlockSpec((tm, w_moba), tok), pl.BlockSpec((tm, w_fox), tok)]
        out_shape += [jax.ShapeDtypeStruct((t, w_moba), F32), jax.ShapeDtypeStruct((t, w_fox), F32)]
    return pl.pallas_call(
        functools.partial(_mix_kernel, w_moba=w_moba, w_fox=w_fox, n_fox=n_fox, seq=seq),
        grid=grid, in_specs=in_specs, out_specs=out_specs, out_shape=out_shape, scratch_shapes=scratch,
        compiler_params=_params(*sem),
        name="mix_seq" if seq else "mix_tok",
    )(h, mod, mod, g.reshape(1, d), cos, sin_signed, wqkv, wf, bf, wg)


def _head_lanes(shape, head):
    lane = lax.broadcasted_iota(jnp.int32, shape, len(shape) - 1)
    return (lane // HEAD_DIM) == head


def _softmax_step(s, v, m, l, acc):
    m_new = jnp.maximum(m, jnp.max(s, axis=1, keepdims=True))
    alpha = jnp.exp(m - m_new)
    p = jnp.exp(s - m_new)
    l = alpha * l + jnp.sum(p, axis=1, keepdims=True)
    acc = alpha * acc + _dot(p.astype(BF16), v)
    return m_new, l, acc


def _moba_prompt_kernel(q_ref, k_ref, v_ref, km_ref, o_ref):
    i = pl.program_id(2)
    tq = q_ref.shape[0]
    nb = km_ref.shape[0]
    q = q_ref[...]
    km = km_ref[...].astype(BF16)
    own0 = pl.multiple_of(i * tq, tq)
    k_own = k_ref[pl.ds(own0, tq), :]
    v_own = v_ref[pl.ds(own0, tq), :]
    rowi = lax.broadcasted_iota(jnp.int32, (tq, tq), 0)
    coli = lax.broadcasted_iota(jnp.int32, (tq, tq), 1)
    blk = lax.broadcasted_iota(jnp.int32, (tq, nb), 1).astype(F32)
    outs = []
    for head in range(HEADS_PER_GROUP):
        qh = jnp.where(_head_lanes(q.shape, head), q, jnp.zeros_like(q))
        gate = jnp.where(blk < i.astype(F32), _dot_nt(qh, km), NEG_INF)
        picks = []
        for _ in range(MOBA_TOPK):
            best = jnp.max(gate, axis=1, keepdims=True)
            pick = jnp.min(jnp.where(gate == best, blk, float(nb)), axis=1, keepdims=True)
            gate = jnp.where(blk == pick, -jnp.inf, gate)
            picks.append(pick)
        s = jnp.where(coli <= rowi, _dot_nt(qh, k_own) * SCALE, NEG_INF)
        m = jnp.max(s, axis=1, keepdims=True)
        p = jnp.exp(s - m)
        l = jnp.sum(p, axis=1, keepdims=True)
        acc = _dot(p.astype(BF16), v_own)

        def body(j, carry):
            off = pl.multiple_of(j * tq, tq)
            kj = k_ref[pl.ds(off, tq), :]
            vj = v_ref[pl.ds(off, tq), :]
            jf = j.astype(F32)
            sel = (picks[0] == jf) | (picks[1] == jf) | (picks[2] == jf)
            sj = jnp.where(sel, _dot_nt(qh, kj) * SCALE, NEG_INF)
            return _softmax_step(sj, vj, *carry)

        m, l, acc = lax.fori_loop(0, i, body, (m, l, acc))
        outs.append(acc / l)
    out = outs[0]
    for head in range(1, HEADS_PER_GROUP):
        out = jnp.where(_head_lanes(out.shape, head), outs[head], out)
    o_ref[...] = out.astype(o_ref.dtype)


def _moba_prompt(q, k, v, km):
    bsz, l, w = q.shape
    nb = km.shape[1]
    tq = MOBA_BLOCK
    return pl.pallas_call(
        _moba_prompt_kernel,
        grid=(bsz, w // LANES, l // tq),
        in_specs=[pl.BlockSpec((None, tq, LANES), lambda b, g, i: (b, i, g)),
                  pl.BlockSpec((None, l, LANES), lambda b, g, i: (b, 0, g)),
                  pl.BlockSpec((None, l, LANES), lambda b, g, i: (b, 0, g)),
                  pl.BlockSpec((None, nb, LANES), lambda b, g, i: (b, 0, g))],
        out_specs=pl.BlockSpec((None, tq, LANES), lambda b, g, i: (b, i, g)),
        out_shape=jax.ShapeDtypeStruct((bsz, l, w), BF16),
        compiler_params=_params("parallel", "parallel", "arbitrary"),
        name="moba_prompt",
    )(q, k, v, km)


def _fox_prompt_kernel(q_ref, k_ref, v_ref, cum_ref, cumt_ref, o_ref):
    g = pl.program_id(1)
    i = pl.program_id(2)
    tq = q_ref.shape[0]
    q = q_ref[...]
    cum = cum_ref[...]
    own0 = pl.multiple_of(i * tq, tq)
    k_own = k_ref[pl.ds(own0, tq), :]
    v_own = v_ref[pl.ds(own0, tq), :]
    rowi = lax.broadcasted_iota(jnp.int32, (tq, tq), 0)
    coli = lax.broadcasted_iota(jnp.int32, (tq, tq), 1)
    hlane = lax.broadcasted_iota(jnp.int32, cum.shape, 1)
    outs = []
    for head in range(HEADS_PER_GROUP):
        qh = jnp.where(_head_lanes(q.shape, head), q, jnp.zeros_like(q))
        qc = jnp.sum(jnp.where(hlane == g * HEADS_PER_GROUP + head, cum, 0.0), axis=1, keepdims=True)
        s = _dot_nt(qh, k_own) * SCALE + (qc - cumt_ref[head:head + 1, pl.ds(own0, tq)])
        s = jnp.where(coli <= rowi, s, NEG_INF)
        m = jnp.max(s, axis=1, keepdims=True)
        p = jnp.exp(s - m)
        l = jnp.sum(p, axis=1, keepdims=True)
        acc = _dot(p.astype(BF16), v_own)

        def body(j, carry):
            off = pl.multiple_of(j * tq, tq)
            kj = k_ref[pl.ds(off, tq), :]
            vj = v_ref[pl.ds(off, tq), :]
            sj = _dot_nt(qh, kj) * SCALE + (qc - cumt_ref[head:head + 1, pl.ds(off, tq)])
            return _softmax_step(sj, vj, *carry)

        m, l, acc = lax.fori_loop(0, i, body, (m, l, acc))
        outs.append(acc / l)
    out = outs[0]
    for head in range(1, HEADS_PER_GROUP):
        out = jnp.where(_head_lanes(out.shape, head), outs[head], out)
    o_ref[...] = out.astype(o_ref.dtype)


def _fox_prompt(q, k, v, cum, cumt):
    bsz, l, w = q.shape
    h = cum.shape[-1]
    tq = MOBA_BLOCK
    groups = w // LANES
    cumt = cumt.reshape(bsz, groups, HEADS_PER_GROUP, l)
    return pl.pallas_call(
        _fox_prompt_kernel,
        grid=(bsz, groups, l // tq),
        in_specs=[pl.BlockSpec((None, tq, LANES), lambda b, g, i: (b, i, g)),
                  pl.BlockSpec((None, l, LANES), lambda b, g, i: (b, 0, g)),
                  pl.BlockSpec((None, l, LANES), lambda b, g, i: (b, 0, g)),
                  pl.BlockSpec((None, tq, h), lambda b, g, i: (b, i, 0)),
                  pl.BlockSpec((None, None, HEADS_PER_GROUP, l), lambda b, g, i: (b, g, 0, 0))],
        out_specs=pl.BlockSpec((None, tq, LANES), lambda b, g, i: (b, i, g)),
        out_shape=jax.ShapeDtypeStruct((bsz, l, w), BF16),
        compiler_params=_params("parallel", "parallel", "arbitrary"),
        name="fox_prompt",
    )(q, k, v, cum, cumt)


PAGES_PER_STEP = 8


def _block_diag_q(q_row, n_heads):
    w = q_row.shape[1]
    qb = jnp.broadcast_to(q_row, (n_heads, w))
    lane_head = lax.broadcasted_iota(jnp.int32, (n_heads, w), 1) // HEAD_DIM
    row = lax.broadcasted_iota(jnp.int32, (n_heads, w), 0)
    return jnp.where(lane_head == row, qb, 0.0), lane_head == row


def _fox_sample_kernel(pt_ref, q_ref, kn_ref, vn_ref, lfn_ref, *rest, n_heads):
    pps = PAGES_PER_STEP
    k_refs, v_refs, lf_refs = rest[:pps], rest[pps:2 * pps], rest[2 * pps:3 * pps]
    o_ref, m_scr, l_scr, acc_scr, carry_scr = rest[3 * pps:]
    c = pl.program_id(1)
    qbd, diag = _block_diag_q(q_ref[...] * SCALE, n_heads)

    @pl.when(c == 0)
    def _():
        m_scr[...] = jnp.sum(qbd * kn_ref[...], axis=1, keepdims=True)
        l_scr[...] = jnp.ones_like(l_scr)
        acc_scr[...] = jnp.broadcast_to(vn_ref[...], acc_scr.shape)
        carry_scr[...] = lfn_ref[...]

    kc = jnp.concatenate([r[...] for r in k_refs], axis=0).astype(BF16)
    vc = jnp.concatenate([r[...] for r in v_refs], axis=0).astype(BF16)
    lf = jnp.concatenate([r[...] for r in lf_refs], axis=1)
    nk = lf.shape[1]
    lane = lax.broadcasted_iota(jnp.int32, lf.shape, 1)
    sfx = jnp.where(lane < nk - 1, pltpu.roll(lf, nk - 1, 1), 0.0)
    d = 1
    while d < nk:
        sfx = sfx + jnp.where(lane < nk - d, pltpu.roll(sfx, nk - d, 1), 0.0)
        d *= 2
    carry = carry_scr[...]
    s = _dot_nt(qbd.astype(BF16), kc) + (carry + sfx)
    carry_scr[...] = carry + jnp.sum(lf, axis=1, keepdims=True)
    m, l, acc = _softmax_step(s, vc, m_scr[...], l_scr[...], acc_scr[...])
    m_scr[...] = m
    l_scr[...] = l
    acc_scr[...] = acc

    @pl.when(c == pl.num_programs(1) - 1)
    def _():
        o_ref[...] = jnp.sum(jnp.where(diag, acc / l, 0.0), axis=0, keepdims=True)


def _fox_sample(page_table, q, k_new, v_new, lf_new, cache_k, cache_v, cache_lft):
    s_n, w = q.shape
    h = lf_new.shape[1]
    n_pages = page_table.shape[1]
    pps = PAGES_PER_STEP
    nc = n_pages // pps

    def page(i):
        return lambda s, c, pt: (pt[s * n_pages + (n_pages - (c + 1) * pps + i)], 0, 0)

    row = pl.BlockSpec((None, 1, w), lambda s, c, pt: (s, 0, 0))
    in_specs = [row, row, row, pl.BlockSpec((None, h, 1), lambda s, c, pt: (s, 0, 0))]
    in_specs += [pl.BlockSpec((None, PAGE_SIZE, w), page(i)) for i in range(pps)] * 2
    in_specs += [pl.BlockSpec((None, h, PAGE_SIZE), page(i)) for i in range(pps)]
    out = pl.pallas_call(
        functools.partial(_fox_sample_kernel, n_heads=h),
        grid_spec=pltpu.PrefetchScalarGridSpec(
            num_scalar_prefetch=1, grid=(s_n, nc), in_specs=in_specs,
            out_specs=pl.BlockSpec((None, 1, w), lambda s, c, pt: (s, 0, 0)),
            scratch_shapes=[pltpu.VMEM((h, 1), F32), pltpu.VMEM((h, 1), F32), pltpu.VMEM((h, w), F32),
                            pltpu.VMEM((h, 1), F32)]),
        out_shape=jax.ShapeDtypeStruct((s_n, 1, w), F32),
        compiler_params=_params("parallel", "arbitrary"),
        name="fox_sample",
    )(page_table.reshape(-1), q.reshape(s_n, 1, w), k_new.reshape(s_n, 1, w), v_new.reshape(s_n, 1, w),
      lf_new.reshape(s_n, h, 1), *([cache_k] * pps), *([cache_v] * pps), *([cache_lft] * pps))
    return out.reshape(s_n, w)


def _moba_sample_kernel(pt_ref, q_ref, kn_ref, vn_ref, *rest, n_heads):
    pps = PAGES_PER_STEP
    k_refs, v_refs = rest[:pps], rest[pps:2 * pps]
    o_ref, s_scr, p_scr, stat_scr, acc_scr = rest[2 * pps:]
    c = pl.program_id(1)
    nc = s_scr.shape[0]
    nk = s_scr.shape[2]
    qbd, diag = _block_diag_q(q_ref[...], n_heads)

    @pl.when(c < nc)
    def _():
        kc = jnp.concatenate([r[...] for r in k_refs], axis=0).astype(BF16)
        s_scr[c] = _dot_nt(qbd.astype(BF16), kc)

    @pl.when(c == nc - 1)
    def _():
        bpc = nk // MOBA_BLOCK
        gates = []
        for cc in range(nc):
            sc = s_scr[cc]
            for b in range(bpc):
                seg = sc[:, b * MOBA_BLOCK:(b + 1) * MOBA_BLOCK]
                gates.append(jnp.broadcast_to(jnp.sum(seg, axis=1, keepdims=True) * (1.0 / MOBA_BLOCK),
                                              seg.shape))
        gate = jnp.concatenate(gates, axis=1)
        blk = (lax.broadcasted_iota(jnp.int32, gate.shape, 1) // MOBA_BLOCK).astype(F32)
        n_blk = float(gate.shape[1] // MOBA_BLOCK)
        sel = jnp.zeros(gate.shape, jnp.bool_)
        for _ in range(MOBA_TOPK):
            best = jnp.max(gate, axis=1, keepdims=True)
            pick = jnp.min(jnp.where(gate == best, blk, n_blk), axis=1, keepdims=True)
            chosen = blk == pick
            sel = sel | chosen
            gate = jnp.where(chosen, -jnp.inf, gate)
        s_self = jnp.sum(qbd * kn_ref[...], axis=1, keepdims=True) * SCALE
        s_all = jnp.concatenate([s_scr[cc] for cc in range(nc)], axis=1) * SCALE
        s_all = jnp.where(sel, s_all, NEG_INF)
        m = jnp.maximum(jnp.max(s_all, axis=1, keepdims=True), s_self)
        p = jnp.exp(s_all - m)
        p_self = jnp.exp(s_self - m)
        stat_scr[:, 0:1] = jnp.sum(p, axis=1, keepdims=True) + p_self
        acc_scr[...] = p_self * jnp.broadcast_to(vn_ref[...], acc_scr.shape)
        for cc in range(nc):
            p_scr[cc] = p[:, cc * nk:(cc + 1) * nk].astype(BF16)

    @pl.when(c >= nc)
    def _():
        vc = jnp.concatenate([r[...] for r in v_refs], axis=0).astype(BF16)
        acc_scr[...] += _dot(p_scr[c - nc], vc)

    @pl.when(c == 2 * nc - 1)
    def _():
        o_ref[...] = jnp.sum(jnp.where(diag, acc_scr[...] / stat_scr[:, 0:1], 0.0), axis=0, keepdims=True)


def _moba_sample(page_table, q, k_new, v_new, cache_k, cache_v):
    s_n, w = q.shape
    h = w // HEAD_DIM
    n_pages = page_table.shape[1]
    pps = PAGES_PER_STEP
    nc = n_pages // pps
    nk = pps * PAGE_SIZE
    assert n_pages * PAGE_SIZE % MOBA_BLOCK == 0 and n_pages * PAGE_SIZE // MOBA_BLOCK >= MOBA_TOPK

    def kpage(i):
        return lambda s, c, pt: (pt[s * n_pages + jnp.minimum(c, nc - 1) * pps + i], 0, 0)

    def vpage(i):
        return lambda s, c, pt: (pt[s * n_pages + jnp.maximum(c - nc, 0) * pps + i], 0, 0)

    row = pl.BlockSpec((None, 1, w), lambda s, c, pt: (s, 0, 0))
    in_specs = [row, row, row]
    in_specs += [pl.BlockSpec((None, PAGE_SIZE, w), kpage(i)) for i in range(pps)]
    in_specs += [pl.BlockSpec((None, PAGE_SIZE, w), vpage(i)) for i in range(pps)]
    out = pl.pallas_call(
        functools.partial(_moba_sample_kernel, n_heads=h),
        grid_spec=pltpu.PrefetchScalarGridSpec(
            num_scalar_prefetch=1, grid=(s_n, 2 * nc), in_specs=in_specs,
            out_specs=pl.BlockSpec((None, 1, w), lambda s, c, pt: (s, 0, 0)),
            scratch_shapes=[pltpu.VMEM((nc, h, nk), F32), pltpu.VMEM((nc, h, nk), BF16),
                            pltpu.VMEM((h, LANES), F32), pltpu.VMEM((h, w), F32)]),
        out_shape=jax.ShapeDtypeStruct((s_n, 1, w), F32),
        compiler_params=_params("parallel", "arbitrary"),
        name="moba_sample",
    )(page_table.reshape(-1), q.reshape(s_n, 1, w), k_new.reshape(s_n, 1, w), v_new.reshape(s_n, 1, w),
      *([cache_k] * pps), *([cache_v] * pps))
    return out.reshape(s_n, w)


def _merge_kernel(h_ref, oa_ref, ob_ref, gl_ref, ga_ref, woa_ref, wob_ref, wout_ref, o_ref):
    d = h_ref.shape[1]
    ya = _dot(oa_ref[...].astype(BF16), woa_ref[...])
    yb = _dot(ob_ref[...].astype(BF16), wob_ref[...])
    gl = gl_ref[...]
    merged = jax.nn.sigmoid(gl[:, :d]) * ya + jax.nn.sigmoid(gl[:, d:]) * yb
    o_ref[...] = h_ref[...] + ga_ref[...] * _dot(merged.astype(BF16), wout_ref[...])


def _merge(h, oa, ob, gl, mod, woa, wob, wout, tokens_per_seq):
    t, d = h.shape
    tm = min(t, 512)
    tps = max(tokens_per_seq // tm, 1)
    tok = lambda i: (i, 0)
    const = lambda i: (0, 0)
    return pl.pallas_call(
        _merge_kernel,
        grid=(t // tm,),
        in_specs=[pl.BlockSpec((tm, d), tok), pl.BlockSpec((tm, oa.shape[1]), tok),
                  pl.BlockSpec((tm, ob.shape[1]), tok), pl.BlockSpec((tm, gl.shape[1]), tok),
                  _mod_spec(mod, 5, tm, tps),
                  pl.BlockSpec(woa.shape, const), pl.BlockSpec(wob.shape, const), pl.BlockSpec(wout.shape, const)],
        out_specs=pl.BlockSpec((tm, d), tok),
        out_shape=jax.ShapeDtypeStruct((t, d), F32),
        compiler_params=_params("parallel"),
        name="merge",
    )(h, oa, ob, gl, mod, woa, wob, wout)


def _rope_tables(pos):
    half = HEAD_DIM // 2
    inv = ROPE_THETA ** (-jnp.arange(half, dtype=F32) / half)
    ang = pos.astype(F32)[:, None] * inv[None, :]
    cos, sin = jnp.cos(ang), jnp.sin(ang)
    reps = LANES // HEAD_DIM
    return (jnp.tile(jnp.concatenate([cos, cos], axis=1), (1, reps)),
            jnp.tile(jnp.concatenate([-sin, sin], axis=1), (1, reps)))


def kernel(x_prompt, x_sample, cache_moba_k, cache_moba_v, cache_fox_k, cache_fox_v, cache_fox_logf, page_table,
           c_prompt, c_sample, w_ada, b_ada, g_ff1, w_ff1_in, w_ff1_out, g_mix, w_mix, b_forget, w_o_moba,
           w_o_fox, w_out, g_ff2, w_ff2_in, w_ff2_out, g_final):
    bsz, l, d = x_prompt.shape
    s_n, dec_seq, _ = x_sample.shape
    assert dec_seq == 1
    depth = w_ada.shape[0]
    n_pool = cache_moba_k.shape[1]
    h_moba, h_fox = cache_moba_k.shape[3], cache_fox_k.shape[3]
    w_moba, w_fox = h_moba * HEAD_DIM, h_fox * HEAD_DIM
    past_len = page_table.shape[1] * PAGE_SIZE
    n_qkv = 3 * w_moba + 3 * w_fox

    cos_p, sin_p = _rope_tables(jnp.arange(l, dtype=jnp.int32))
    cos_s, sin_s = _rope_tables(jnp.full((1,), past_len, jnp.int32))
    pad_rows = (-(s_n + bsz)) % 8
    c_all = jnp.concatenate([c_sample, c_prompt, jnp.zeros((pad_rows, d), F32)], axis=0)

    hp = x_prompt.reshape(bsz * l, d)
    hs = x_sample.reshape(s_n, d)
    st_p = [[] for _ in range(5)]
    st_s = [[] for _ in range(5)]
    for layer in range(depth):
        mod = _modulation(c_all, w_ada[layer], b_ada[layer])
        mod_s = mod[:, :s_n]
        mod_p = mod[:, s_n:s_n + bsz].reshape(N_MOD, bsz, 1, d)
        w1i, w1o = w_ff1_in[layer].astype(BF16), w_ff1_out[layer].astype(BF16)
        w2i, w2o = w_ff2_in[layer].astype(BF16), w_ff2_out[layer].astype(BF16)
        wm = w_mix[layer]
        wqkv = wm[:, :n_qkv].astype(BF16)
        wf = jnp.pad(wm[:, n_qkv:n_qkv + h_fox], ((0, 0), (0, LANES - h_fox))).astype(BF16)
        wg = wm[:, n_qkv + h_fox:].astype(BF16)
        bfor = jnp.pad(b_forget[layer], (0, LANES - h_fox)).reshape(1, LANES)
        woa, wob, wout = w_o_moba[layer].astype(BF16), w_o_fox[layer].astype(BF16), w_out[layer].astype(BF16)
        last = layer == depth - 1
        mix_kw = dict(w_moba=w_moba, w_fox=w_fox, n_fox=h_fox)

        hp = _ffn(hp, mod_p, 0, g_ff1[layer], w1i, w1o, l)
        (ka, va, kb, vb, lf, gl, qa16, ka16, va16, qb16, kb16, vb16, cum, cumt, km) = _mix(
            hp, mod_p, g_mix[layer], cos_p, sin_p, wqkv, wf, bfor, wg, seq_shape=(bsz, l), **mix_kw)
        to3 = lambda a: a.reshape(bsz, l, -1)
        oa = _moba_prompt(to3(qa16), to3(ka16), to3(va16), km.reshape(bsz, l // MOBA_BLOCK, w_moba))
        ob = _fox_prompt(to3(qb16), to3(kb16), to3(vb16), to3(cum), cumt)
        hp = _merge(hp, oa.reshape(bsz * l, w_moba), ob.reshape(bsz * l, w_fox), gl, mod_p, woa, wob, wout, l)
        hp = _ffn(hp, mod_p, 6, g_ff2[layer], w2i, w2o, l, g_final=g_final if last else None)
        for i, a in enumerate((ka.reshape(bsz, l, h_moba, HEAD_DIM), va.reshape(bsz, l, h_moba, HEAD_DIM),
                               kb.reshape(bsz, l, h_fox, HEAD_DIM), vb.reshape(bsz, l, h_fox, HEAD_DIM),
                               lf.reshape(bsz, l, h_fox))):
            st_p[i].append(a)

        hs = _ffn(hs, mod_s, 0, g_ff1[layer], w1i, w1o, 1)
        ka, va, kb, vb, lf, gl, qa, qb = _mix(hs, mod_s, g_mix[layer], cos_s, sin_s, wqkv, wf, bfor, wg,
                                              seq_shape=None, **mix_kw)
        oa = _moba_sample(page_table, qa, ka, va,
                          cache_moba_k[layer].reshape(n_pool, PAGE_SIZE, w_moba),
                          cache_moba_v[layer].reshape(n_pool, PAGE_SIZE, w_moba))
        ob = _fox_sample(page_table, qb, kb, vb, lf,
                         cache_fox_k[layer].reshape(n_pool, PAGE_SIZE, w_fox),
                         cache_fox_v[layer].reshape(n_pool, PAGE_SIZE, w_fox),
                         jnp.swapaxes(cache_fox_logf[layer], 1, 2))
        hs = _merge(hs, oa, ob, gl, mod_s, woa, wob, wout, 1)
        hs = _ffn(hs, mod_s, 6, g_ff2[layer], w2i, w2o, 1, g_final=g_final if last else None)
        for i, a in enumerate((ka.reshape(s_n, 1, h_moba, HEAD_DIM), va.reshape(s_n, 1, h_moba, HEAD_DIM),
                               kb.reshape(s_n, 1, h_fox, HEAD_DIM), vb.reshape(s_n, 1, h_fox, HEAD_DIM),
                               lf.reshape(s_n, 1, h_fox))):
            st_s[i].append(a)

    return (hp.reshape(bsz, l, d), hs.reshape(s_n, 1, d),
            *(jnp.stack(a) for a in st_p), *(jnp.stack(a) for a in st_s))
```

```python
import functools
import math

import jax
import jax.numpy as jnp
from jax import lax
from jax.experimental import pallas as pl
from jax.experimental.pallas import tpu as pltpu

HEAD_DIM = 64
MOBA_BLOCK = 256
MOBA_TOPK = 3
PAGE_SIZE = 128
ROPE_THETA = 10000.0
NORM_EPS = 1e-6
NEG_INF = -1e30
N_MOD = 9
LANES = 128
HEADS_PER_GROUP = LANES // HEAD_DIM
VMEM_LIMIT = 56 * 1024 * 1024

F32 = jnp.float32
BF16 = jnp.bfloat16
SCALE = 1.0 / math.sqrt(HEAD_DIM)

N_PIECES = 3
MOBA_LANE0 = 32


def _params(*sem):
    return pltpu.CompilerParams(dimension_semantics=sem, vmem_limit_bytes=VMEM_LIMIT)


def _dot(a, b):
    return jnp.dot(a, b, preferred_element_type=F32)


def _dot_nt(a, b):
    return lax.dot_general(a, b, (((1,), (1,)), ((), ())), preferred_element_type=F32)


def _rms(x, g):
    return x * lax.rsqrt(jnp.mean(x * x, axis=-1, keepdims=True) + NORM_EPS) * g


def _silu(x):
    return x * jax.nn.sigmoid(x)


def _split_bf16(x):
    hi = x.astype(BF16).astype(F32)
    r = x - hi
    mid = r.astype(BF16).astype(F32)
    lo = (r - mid).astype(BF16).astype(F32)
    return hi, mid, lo


def _mod_kernel(c_ref, w_ref, b_ref, o_ref):
    a = _silu(c_ref[...]).astype(BF16)
    o_ref[...] = _dot(a, w_ref[...].astype(BF16)) + b_ref[...]


def _modulation(c, w_ada, b_ada):
    r, d = c.shape
    return pl.pallas_call(
        _mod_kernel,
        grid=(N_MOD,),
        in_specs=[pl.BlockSpec((r, d), lambda j: (0, 0)),
                  pl.BlockSpec((d, d), lambda j: (0, j)),
                  pl.BlockSpec((1, d), lambda j: (0, j))],
        out_specs=pl.BlockSpec((None, r, d), lambda j: (j, 0, 0)),
        out_shape=jax.ShapeDtypeStruct((N_MOD, r, d), F32),
        compiler_params=_params("arbitrary"),
        name="mod",
    )(c, w_ada, b_ada.reshape(1, -1))


def _mod_spec(mod, k, tm, tiles_per_seq):
    d = mod.shape[-1]
    if mod.ndim == 4:
        return pl.BlockSpec((None, None, 1, d), lambda i, *_: (k, i // tiles_per_seq, 0, 0))
    return pl.BlockSpec((None, tm, d), lambda i, *_: (k, i, 0))


def _ffn_kernel(x_ref, sh_ref, sc_ref, ga_ref, g_ref, wa_ref, wb_ref, wo_ref, *rest, final_norm):
    if final_norm:
        gf_ref, o_ref, n_scr, acc_scr = rest
    else:
        o_ref, n_scr, acc_scr = rest
    c = pl.program_id(1)

    @pl.when(c == 0)
    def _():
        n = _rms(x_ref[...], g_ref[...]) * (1 + sc_ref[...]) + sh_ref[...]
        n_scr[...] = n.astype(BF16)
        acc_scr[...] = jnp.zeros_like(acc_scr)

    n = n_scr[...]
    a = _dot(n, wa_ref[...])
    b = _dot(n, wb_ref[...])
    acc_scr[...] += _dot((_silu(a) * b).astype(BF16), wo_ref[...])

    @pl.when(c == pl.num_programs(1) - 1)
    def _():
        h = x_ref[...] + 0.5 * ga_ref[...] * acc_scr[...]
        if final_norm:
            h = _rms(h, gf_ref[...])
        o_ref[...] = h


def _ffn_tiles(t, d_ff):
    tm = min(t, 512)
    tf = d_ff // 2 if (d_ff // 2) % LANES == 0 else d_ff
    return tm, tf


def _ffn(x, mod, k0, g, w_in, w_out, tiles_per_seq_tokens, g_final=None):
    t, d = x.shape
    d_ff = w_out.shape[0]
    tm, tf = _ffn_tiles(t, d_ff)
    nc = d_ff // tf
    tps = max(tiles_per_seq_tokens // tm, 1)
    in_specs = [pl.BlockSpec((tm, d), lambda i, c: (i, 0)),
                _mod_spec(mod, k0, tm, tps), _mod_spec(mod, k0 + 1, tm, tps), _mod_spec(mod, k0 + 2, tm, tps),
                pl.BlockSpec((1, d), lambda i, c: (0, 0)),
                pl.BlockSpec((d, tf), lambda i, c: (0, c)),
                pl.BlockSpec((d, tf), lambda i, c: (0, c + nc)),
                pl.BlockSpec((tf, d), lambda i, c: (c, 0))]
    args = [x, mod, mod, mod, g.reshape(1, d), w_in, w_in, w_out]
    if g_final is not None:
        in_specs.append(pl.BlockSpec((1, d), lambda i, c: (0, 0)))
        args.append(g_final.reshape(1, d))
    return pl.pallas_call(
        functools.partial(_ffn_kernel, final_norm=g_final is not None),
        grid=(t // tm, nc),
        in_specs=in_specs,
        out_specs=pl.BlockSpec((tm, d), lambda i, c: (i, 0)),
        out_shape=jax.ShapeDtypeStruct((t, d), F32),
        scratch_shapes=[pltpu.VMEM((tm, d), BF16), pltpu.VMEM((tm, d), F32)],
        compiler_params=_params("parallel", "arbitrary"),
        name="ffn_final" if g_final is not None else "ffn",
    )(*args)


def _rope(x, cos, sin_signed):
    w = x.shape[1]
    reps = w // LANES
    cos = jnp.concatenate([cos] * reps, axis=1)
    sin_signed = jnp.concatenate([sin_signed] * reps, axis=1)
    lane = lax.broadcasted_iota(jnp.int32, x.shape, 1)
    first_half = (lane % HEAD_DIM) < (HEAD_DIM // 2)
    swapped = jnp.where(first_half, pltpu.roll(x, w - HEAD_DIM // 2, 1), pltpu.roll(x, HEAD_DIM // 2, 1))
    return x * cos + swapped * sin_signed


def _mix_kernel(h_ref, sh_ref, sc_ref, g_ref, cos_ref, sin_ref, wqkv_ref, wf_ref, bf_ref, wg_ref, *rest,
                w_moba, w_fox, n_fox, seq):
    if seq:
        (kat_ref, vat_ref, kbt_ref, vbt_ref, lft_ref, gl_ref, ka16_ref, kb16_ref, qat16_ref, vat16_ref, qbt16_ref,
         vbt16_ref, kbias_ref, cumt_ref, km_ref, carry_scr) = rest
    else:
        ka_ref, va_ref, kb_ref, vb_ref, lf_ref, gl_ref, qa_ref, qb_ref = rest
    n = (_rms(h_ref[...], g_ref[...]) * (1 + sc_ref[...]) + sh_ref[...]).astype(BF16)
    z = _dot(n, wqkv_ref[...])
    o = 0
    qa = _rope(z[:, o:o + w_moba], cos_ref[...], sin_ref[...]); o += w_moba
    ka = _rope(z[:, o:o + w_moba], cos_ref[...], sin_ref[...]); o += w_moba
    va = z[:, o:o + w_moba]; o += w_moba
    qb = z[:, o:o + w_fox]; o += w_fox
    kb = z[:, o:o + w_fox]; o += w_fox
    vb = z[:, o:o + w_fox]
    lf = jax.nn.log_sigmoid(_dot(n, wf_ref[...]) + bf_ref[...])
    gl_ref[...] = _dot(n, wg_ref[...])
    if not seq:
        ka_ref[...] = ka
        va_ref[...] = va
        kb_ref[...] = kb
        vb_ref[...] = vb
        lf_ref[...] = lf[:, :n_fox]
        qa_ref[...] = qa
        qb_ref[...] = qb
        return
    kat, vat, kbt, vbt = ka.T, va.T, kb.T, vb.T
    kat_ref[...] = kat
    vat_ref[...] = vat
    kbt_ref[...] = kbt
    vbt_ref[...] = vbt
    lft_ref[...] = lf.T[:n_fox, :]
    ka16_ref[...] = ka.astype(BF16)
    kb16_ref[...] = kb.astype(BF16)
    qat16_ref[...] = (qa * SCALE).T.astype(BF16)
    vat16_ref[...] = vat.astype(BF16)
    qbt16_ref[...] = (qb * SCALE).T.astype(BF16)
    vbt16_ref[...] = vbt.astype(BF16)
    tm = lf.shape[0]

    @pl.when(pl.program_id(1) == 0)
    def _():
        carry_scr[...] = jnp.zeros_like(carry_scr)

    row = lax.broadcasted_iota(jnp.int32, (tm, tm), 0)
    col = lax.broadcasted_iota(jnp.int32, (tm, tm), 1)
    tri = (row >= col).astype(F32)
    cum = jnp.dot(tri, lf, precision=lax.Precision.HIGHEST, preferred_element_type=F32) + carry_scr[...]
    carry_scr[...] = cum[tm - 1:tm, :]
    cumt_ref[...] = cum.T[:n_fox, :]
    hi, mid, lo = _split_bf16(cum)
    lane = lax.broadcasted_iota(jnp.int32, cum.shape, 1)
    kbias = jnp.where(lane < n_fox, -hi,
                      jnp.where(lane < 2 * n_fox, -pltpu.roll(mid, n_fox, 1),
                                jnp.where(lane < 3 * n_fox, -pltpu.roll(lo, 2 * n_fox, 1),
                                          jnp.where(lane < 3 * n_fox + N_PIECES, 1.0, 0.0))))
    kbias_ref[...] = kbias.astype(BF16)
    km_ref[...] = jnp.sum(ka, axis=0, keepdims=True) * (1.0 / MOBA_BLOCK)


def _mix(h, mod, g, cos, sin_signed, wqkv, wf, bf, wg, *, w_moba, w_fox, n_fox, seq_shape):
    t, d = h.shape
    seq = seq_shape is not None
    tm = min(t, MOBA_BLOCK)
    if seq:
        bsz, l = seq_shape
        nt = l // tm
        grid = (bsz, nt)
        tok = lambda b, i: (b * nt + i, 0)
        rope_spec = pl.BlockSpec((tm, LANES), lambda b, i: (i, 0))
        mod_specs = [pl.BlockSpec((None, None, 1, d), functools.partial(lambda b, i, k: (k, b, 0, 0), k=k))
                     for k in (3, 4)]
        sem = ("parallel", "arbitrary")
    else:
        grid = (t // tm,)
        tok = lambda i: (i, 0)
        rope_spec = pl.BlockSpec((1, LANES), lambda i: (0, 0))
        mod_specs = [pl.BlockSpec((None, tm, d), functools.partial(lambda i, k: (k, i, 0), k=k)) for k in (3, 4)]
        sem = ("parallel",)
    const = lambda *_: (0, 0)
    in_specs = [pl.BlockSpec((tm, d), tok), *mod_specs,
                pl.BlockSpec((1, d), const), rope_spec, rope_spec,
                pl.BlockSpec(wqkv.shape, const), pl.BlockSpec(wf.shape, const), pl.BlockSpec(bf.shape, const),
                pl.BlockSpec(wg.shape, const)]
    scratch = []
    if seq:
        assert tm == MOBA_BLOCK and 3 * n_fox + N_PIECES <= LANES
        tr = lambda b, i: (b, 0, i)
        out_specs = [pl.BlockSpec((None, w_moba, tm), tr), pl.BlockSpec((None, w_moba, tm), tr),
                     pl.BlockSpec((None, w_fox, tm), tr), pl.BlockSpec((None, w_fox, tm), tr),
                     pl.BlockSpec((None, n_fox, tm), tr), pl.BlockSpec((tm, wg.shape[1]), tok),
                     pl.BlockSpec((tm, w_moba), tok), pl.BlockSpec((tm, w_fox), tok),
                     pl.BlockSpec((None, w_moba, tm), tr), pl.BlockSpec((None, w_moba, tm), tr),
                     pl.BlockSpec((None, w_fox, tm), tr), pl.BlockSpec((None, w_fox, tm), tr),
                     pl.BlockSpec((tm, LANES), tok),
                     pl.BlockSpec((None, n_fox, tm), tr),
                     pl.BlockSpec((None, None, 1, w_moba), lambda b, i: (b, i, 0, 0))]
        out_shape = [jax.ShapeDtypeStruct((bsz, w_moba, l), F32), jax.ShapeDtypeStruct((bsz, w_moba, l), F32),
                     jax.ShapeDtypeStruct((bsz, w_fox, l), F32), jax.ShapeDtypeStruct((bsz, w_fox, l), F32),
                     jax.ShapeDtypeStruct((bsz, n_fox, l), F32), jax.ShapeDtypeStruct((t, wg.shape[1]), F32),
                     jax.ShapeDtypeStruct((t, w_moba), BF16), jax.ShapeDtypeStruct((t, w_fox), BF16),
                     jax.ShapeDtypeStruct((bsz, w_moba, l), BF16), jax.ShapeDtypeStruct((bsz, w_moba, l), BF16),
                     jax.ShapeDtypeStruct((bsz, w_fox, l), BF16), jax.ShapeDtypeStruct((bsz, w_fox, l), BF16),
                     jax.ShapeDtypeStruct((t, LANES), BF16),
                     jax.ShapeDtypeStruct((bsz, n_fox, l), F32),
                     jax.ShapeDtypeStruct((bsz, nt, 1, w_moba), F32)]
        scratch = [pltpu.VMEM((1, LANES), F32)]
    else:
        out_specs = [pl.BlockSpec((tm, w_moba), tok), pl.BlockSpec((tm, w_moba), tok),
                     pl.BlockSpec((tm, w_fox), tok), pl.BlockSpec((tm, w_fox), tok),
                     pl.BlockSpec((tm, n_fox), tok), pl.BlockSpec((tm, wg.shape[1]), tok),
                     pl.BlockSpec((tm, w_moba), tok), pl.BlockSpec((tm, w_fox), tok)]
        out_shape = [jax.ShapeDtypeStruct((t, w_moba), F32), jax.ShapeDtypeStruct((t, w_moba), F32),
                     jax.ShapeDtypeStruct((t, w_fox), F32), jax.ShapeDtypeStruct((t, w_fox), F32),
                     jax.ShapeDtypeStruct((t, n_fox), F32), jax.ShapeDtypeStruct((t, wg.shape[1]), F32),
                     jax.ShapeDtypeStruct((t, w_moba), F32), jax.ShapeDtypeStruct((t, w_fox), F32)]
    return pl.pallas_call(
        functools.partial(_mix_kernel, w_moba=w_moba, w_fox=w_fox, n_fox=n_fox, seq=seq),
        grid=grid, in_specs=in_specs, out_specs=out_specs, out_shape=out_shape, scratch_shapes=scratch,
        compiler_params=_params(*sem),
        name="mix_seq" if seq else "mix_tok",
    )(h, mod, mod, g.reshape(1, d), cos, sin_signed, wqkv, wf, bf, wg)


def _prompt_attn_kernel(qt_ref, k_ref, vt_ref, kbias_ref, side_ref, o_ref, *scratch, moba, n_fox):
    g = pl.program_id(1)
    i = pl.program_id(2)
    tq = qt_ref.shape[1]
    q0 = pl.multiple_of(i * tq, tq)
    qt = qt_ref[...]
    hpg = HEADS_PER_GROUP
    s_bufs = [scratch[b * hpg:(b + 1) * hpg] for b in range(2)]
    mx_bufs = [scratch[(2 + b) * hpg:(3 + b) * hpg] for b in range(2)]
    wrow = lax.broadcasted_iota(jnp.int32, (LANES, tq), 0)
    ones_rows = jnp.ones((16, 1), BF16)

    weights = []
    for head in range(HEADS_PER_GROUP):
        w_top = jnp.where(wrow // HEAD_DIM == head, qt, jnp.zeros_like(qt))
        if moba:
            km = side_ref[...].astype(BF16)
            kmh = jnp.where(lax.broadcasted_iota(jnp.int32, km.shape, 1) // HEAD_DIM == head, km, jnp.zeros_like(km))
            blk = (wrow - MOBA_LANE0).astype(F32)
            gate = jnp.where((wrow >= MOBA_LANE0) & (wrow < MOBA_LANE0 + i), _dot(kmh, qt), NEG_INF)
            sel = jnp.zeros(gate.shape, jnp.bool_)
            for _ in range(MOBA_TOPK):
                best = jnp.max(gate, axis=0, keepdims=True)
                pick = jnp.min(jnp.where(gate == best, blk, float(LANES)), axis=0, keepdims=True)
                chosen = blk == pick
                sel = sel | chosen
                gate = jnp.where(chosen, -jnp.inf, gate)
            w_bot = jnp.where((wrow >= MOBA_LANE0) & (wrow < MOBA_LANE0 + i) & jnp.logical_not(sel), NEG_INF, 0.0)
        else:
            hg = g * HEADS_PER_GROUP + head
            qc = side_ref[head:head + 1, pl.ds(q0, tq)]
            hi, mid, lo = _split_bf16(qc)
            w_bot = jnp.where((wrow < N_PIECES * n_fox) & (wrow % n_fox == hg), 1.0,
                              jnp.where(wrow == N_PIECES * n_fox, hi,
                                        jnp.where(wrow == N_PIECES * n_fox + 1, mid,
                                                  jnp.where(wrow == N_PIECES * n_fox + 2, lo, 0.0))))
        weights.append(jnp.concatenate([w_top, w_bot.astype(BF16)], axis=0))

    def scores(off, tk, head):
        kaug = jnp.concatenate([k_ref[pl.ds(off, tk), :], kbias_ref[pl.ds(off, tk), :]], axis=1)
        return _dot(kaug, weights[head])

    def values(off, tk, head):
        vh = vt_ref[head * HEAD_DIM:(head + 1) * HEAD_DIM, pl.ds(off, tk)]
        return jnp.concatenate([vh, jnp.broadcast_to(ones_rows, (16, tk))], axis=0)

    krow = lax.broadcasted_iota(jnp.int32, (tq, tq), 0)
    qcol = lax.broadcasted_iota(jnp.int32, (tq, tq), 1)
    carry = []
    for head in range(HEADS_PER_GROUP):
        st = jnp.where(krow <= qcol, scores(q0, tq, head), NEG_INF)
        m = jnp.max(st, axis=0, keepdims=True)
        p = jnp.exp(st - m).astype(BF16)
        carry += [m, _dot(values(q0, tq, head), p)]

    def produce(blk, buf):
        off = pl.multiple_of(blk * tq, tq)
        for head in range(HEADS_PER_GROUP):
            st = scores(off, tq, head)
            s_bufs[buf][head][...] = st
            mx_bufs[buf][head][...] = jnp.max(st.reshape(tq // 8, 8, tq), axis=0)

    def consume(blk, buf, carry):
        off = pl.multiple_of(blk * tq, tq)
        out = []
        for head in range(HEADS_PER_GROUP):
            m, acc = carry[2 * head], carry[2 * head + 1]
            m_new = jnp.maximum(m, jnp.max(mx_bufs[buf][head][...], axis=0, keepdims=True))
            p = jnp.exp(s_bufs[buf][head][...] - m_new).astype(BF16)
            out += [m_new, jnp.exp(m - m_new) * acc + _dot(values(off, tq, head), p)]
        return out

    def pair(t, carry):
        j = 2 * t
        produce(j + 1, 1)
        carry = consume(j, 0, carry)
        produce(j + 2, 0)
        return consume(j + 1, 1, carry)

    produce(0, 0)
    carry = lax.fori_loop(0, i // 2, pair, carry)
    carry = lax.cond(i % 2 == 1, lambda c: consume(i - 1, 0, c), lambda c: c, carry)

    outs = [carry[2 * head + 1][:HEAD_DIM] / carry[2 * head + 1][HEAD_DIM:HEAD_DIM + 1]
            for head in range(HEADS_PER_GROUP)]
    o_ref[...] = jnp.concatenate(outs, axis=0).T.astype(o_ref.dtype)


def _prompt_attn(qt, k, vt, kbias, side, *, moba, n_fox):
    bsz, w, l = qt.shape
    tq = MOBA_BLOCK
    groups = w // LANES
    if moba:
        kbias_spec = pl.BlockSpec((l, LANES), lambda b, g, i: (0, 0))
        side_spec = pl.BlockSpec((None, LANES, LANES), lambda b, g, i: (b, 0, g))
    else:
        kbias_spec = pl.BlockSpec((None, l, LANES), lambda b, g, i: (b, 0, 0))
        side = side.reshape(bsz, groups, HEADS_PER_GROUP, l)
        side_spec = pl.BlockSpec((None, None, HEADS_PER_GROUP, l), lambda b, g, i: (b, g, 0, 0))
    return pl.pallas_call(
        functools.partial(_prompt_attn_kernel, moba=moba, n_fox=n_fox),
        grid=(bsz, groups, l // tq),
        in_specs=[pl.BlockSpec((None, LANES, tq), lambda b, g, i: (b, g, i)),
                  pl.BlockSpec((None, l, LANES), lambda b, g, i: (b, 0, g)),
                  pl.BlockSpec((None, LANES, l), lambda b, g, i: (b, g, 0)),
                  kbias_spec, side_spec],
        out_specs=pl.BlockSpec((None, tq, LANES), lambda b, g, i: (b, i, g)),
        out_shape=jax.ShapeDtypeStruct((bsz, l, w), BF16),
        scratch_shapes=([pltpu.VMEM((tq, tq), F32)] * (2 * HEADS_PER_GROUP)
                        + [pltpu.VMEM((8, tq), F32)] * (2 * HEADS_PER_GROUP)),
        compiler_params=_params("parallel", "parallel", "arbitrary"),
        name="moba_prompt" if moba else "fox_prompt",
    )(qt, k, vt, kbias, side)


PAGES_PER_STEP = 16


def _lane_bcast(col, n):
    return jnp.broadcast_to(col, (col.shape[0], n))


def _page_scores(qb_scr, k_refs):
    rows = []
    for head in range(qb_scr.shape[0]):
        qh = qb_scr[head]
        rows.append(jnp.concatenate([jnp.sum(r[head] * qh, axis=0, keepdims=True) for r in k_refs], axis=1))
    return jnp.concatenate(rows, axis=0)


def _accumulate_values(acc_scr, v_refs, p, alpha):
    page = acc_scr.shape[2]
    for head in range(acc_scr.shape[0]):
        a = acc_scr[head]
        if alpha is not None:
            a = a * alpha[head:head + 1, :]
        for i, r in enumerate(v_refs):
            a = a + r[head] * p[head:head + 1, i * page:(i + 1) * page]
        acc_scr[head] = a


def _finish_rows(acc_scr, l):
    h, hd, page = acc_scr.shape
    parts = [acc_scr[head] / l[head:head + 1, :] for head in range(h)]
    return jnp.sum(jnp.concatenate(parts, axis=0).T, axis=0, keepdims=True)


def _new_token_values(vnt_ref, acc_scr, weight):
    h, hd, page = acc_scr.shape
    lane0 = lax.broadcasted_iota(jnp.int32, (hd, page), 1) == 0
    for head in range(h):
        col = vnt_ref[:, head:head + 1]
        if weight is not None:
            col = col * weight[head:head + 1, :]
        acc_scr[head] = jnp.where(lane0, _lane_bcast(col, page), 0.0)


def _fox_sample_kernel(pt_ref, q_ref, qt_ref, kn_ref, vnt_ref, lfn_ref, *rest):
    pps = PAGES_PER_STEP
    k_refs, v_refs, lf_refs = rest[:pps], rest[pps:2 * pps], rest[2 * pps:3 * pps]
    o_ref, qb_scr, m_scr, l_scr, acc_scr, carry_scr = rest[3 * pps:]
    c = pl.program_id(1)
    page = acc_scr.shape[2]

    @pl.when(c == 0)
    def _():
        for head in range(qb_scr.shape[0]):
            qb_scr[head] = _lane_bcast(qt_ref[:, head:head + 1] * SCALE, page)
        m_scr[...] = jnp.sum(q_ref[...] * SCALE * kn_ref[...], axis=1, keepdims=True)
        l_scr[...] = jnp.ones_like(l_scr)
        _new_token_values(vnt_ref, acc_scr, None)
        carry_scr[...] = lfn_ref[...]

    lf = jnp.concatenate([r[...] for r in lf_refs], axis=1)
    nk = lf.shape[1]
    lane = lax.broadcasted_iota(jnp.int32, lf.shape, 1)
    sfx = jnp.where(lane < nk - 1, pltpu.roll(lf, nk - 1, 1), 0.0)
    d = 1
    while d < nk:
        sfx = sfx + jnp.where(lane < nk - d, pltpu.roll(sfx, nk - d, 1), 0.0)
        d *= 2
    carry = carry_scr[...]
    s = _page_scores(qb_scr, k_refs) + (carry + sfx)
    carry_scr[...] = carry + jnp.sum(lf, axis=1, keepdims=True)
    m = m_scr[...]
    m_new = jnp.maximum(m, jnp.max(s, axis=1, keepdims=True))
    alpha = jnp.exp(m - m_new)
    p = jnp.exp(s - m_new)
    m_scr[...] = m_new
    l_scr[...] = alpha * l_scr[...] + jnp.sum(p, axis=1, keepdims=True)
    _accumulate_values(acc_scr, v_refs, p, alpha)

    @pl.when(c == pl.num_programs(1) - 1)
    def _():
        o_ref[...] = _finish_rows(acc_scr, l_scr[...])


def _sample_row_specs(h, hd):
    nat = pl.BlockSpec((None, h, hd), lambda s, c, pt: (s, 0, 0))
    tr = pl.BlockSpec((None, hd, h), lambda s, c, pt: (s, 0, 0))
    return nat, tr


def _fox_sample(page_table, q, k_new, v_new, lf_new, cache_kt, cache_vt, cache_lft):
    s_n, h, hd = q.shape
    n_pages = page_table.shape[1]
    pps = PAGES_PER_STEP
    assert n_pages % pps == 0
    nc = n_pages // pps

    def page(i, nd):
        return lambda s, c, pt: (pt[s * n_pages + (n_pages - (c + 1) * pps + i)],) + (0,) * nd

    nat, tr = _sample_row_specs(h, hd)
    in_specs = [nat, tr, nat, tr, pl.BlockSpec((None, h, 1), lambda s, c, pt: (s, 0, 0))]
    in_specs += [pl.BlockSpec((None, h, hd, PAGE_SIZE), page(i, 3)) for i in range(pps)]
    in_specs += [pl.BlockSpec((None, h, hd, PAGE_SIZE), page(i, 3)) for i in range(pps)]
    in_specs += [pl.BlockSpec((None, h, PAGE_SIZE), page(i, 2)) for i in range(pps)]
    out = pl.pallas_call(
        _fox_sample_kernel,
        grid_spec=pltpu.PrefetchScalarGridSpec(
            num_scalar_prefetch=1, grid=(s_n, nc), in_specs=in_specs,
            out_specs=pl.BlockSpec((None, 1, h * hd), lambda s, c, pt: (s, 0, 0)),
            scratch_shapes=[pltpu.VMEM((h, hd, PAGE_SIZE), F32), pltpu.VMEM((h, 1), F32), pltpu.VMEM((h, 1), F32),
                            pltpu.VMEM((h, hd, PAGE_SIZE), F32), pltpu.VMEM((h, 1), F32)]),
        out_shape=jax.ShapeDtypeStruct((s_n, 1, h * hd), F32),
        compiler_params=_params("parallel", "arbitrary"),
        name="fox_sample",
    )(page_table.reshape(-1), q, jnp.swapaxes(q, 1, 2), k_new, jnp.swapaxes(v_new, 1, 2),
      lf_new.reshape(s_n, h, 1), *([cache_kt] * pps), *([cache_vt] * pps), *([cache_lft] * pps))
    return out.reshape(s_n, h * hd)


def _moba_sample_kernel(pt_ref, q_ref, qt_ref, kn_ref, vnt_ref, *rest):
    pps = PAGES_PER_STEP
    k_refs, v_refs = rest[:pps], rest[pps:2 * pps]
    o_ref, qb_scr, s_scr, p_scr, l_scr, acc_scr = rest[2 * pps:]
    c = pl.program_id(1)
    nc = s_scr.shape[0]
    nk = s_scr.shape[2]
    page = acc_scr.shape[2]

    @pl.when(c == 0)
    def _():
        for head in range(qb_scr.shape[0]):
            qb_scr[head] = _lane_bcast(qt_ref[:, head:head + 1], page)

    @pl.when(c < nc)
    def _():
        s_scr[c] = _page_scores(qb_scr, k_refs)

    @pl.when(c == nc - 1)
    def _():
        bpc = nk // MOBA_BLOCK
        gates = []
        for cc in range(nc):
            sc = s_scr[cc]
            for b in range(bpc):
                seg = sc[:, b * MOBA_BLOCK:(b + 1) * MOBA_BLOCK]
                gates.append(jnp.broadcast_to(jnp.sum(seg, axis=1, keepdims=True) * (1.0 / MOBA_BLOCK),
                                              seg.shape))
        gate = jnp.concatenate(gates, axis=1)
        blk = (lax.broadcasted_iota(jnp.int32, gate.shape, 1) // MOBA_BLOCK).astype(F32)
        n_blk = float(gate.shape[1] // MOBA_BLOCK)
        sel = jnp.zeros(gate.shape, jnp.bool_)
        for _ in range(MOBA_TOPK):
            best = jnp.max(gate, axis=1, keepdims=True)
            pick = jnp.min(jnp.where(gate == best, blk, n_blk), axis=1, keepdims=True)
            chosen = blk == pick
            sel = sel | chosen
            gate = jnp.where(chosen, -jnp.inf, gate)
        s_self = jnp.sum(q_ref[...] * kn_ref[...], axis=1, keepdims=True) * SCALE
        s_all = jnp.concatenate([s_scr[cc] for cc in range(nc)], axis=1) * SCALE
        s_all = jnp.where(sel, s_all, NEG_INF)
        m = jnp.maximum(jnp.max(s_all, axis=1, keepdims=True), s_self)
        p = jnp.exp(s_all - m)
        p_self = jnp.exp(s_self - m)
        l_scr[...] = jnp.sum(p, axis=1, keepdims=True) + p_self
        _new_token_values(vnt_ref, acc_scr, p_self)
        for cc in range(nc):
            p_scr[cc] = p[:, cc * nk:(cc + 1) * nk]

    @pl.when(c >= nc)
    def _():
        _accumulate_values(acc_scr, v_refs, p_scr[c - nc], None)

    @pl.when(c == 2 * nc - 1)
    def _():
        o_ref[...] = _finish_rows(acc_scr, l_scr[...])


def _moba_sample(page_table, q, k_new, v_new, cache_kt, cache_vt):
    s_n, h, hd = q.shape
    n_pages = page_table.shape[1]
    pps = PAGES_PER_STEP
    assert n_pages % pps == 0
    nc = n_pages // pps
    nk = pps * PAGE_SIZE
    assert nk % MOBA_BLOCK == 0 and n_pages * PAGE_SIZE // MOBA_BLOCK >= MOBA_TOPK

    def kpage(i):
        return lambda s, c, pt: (pt[s * n_pages + jnp.minimum(c, nc - 1) * pps + i],) + (0,) * 3

    def vpage(i):
        return lambda s, c, pt: (pt[s * n_pages + jnp.maximum(c - nc, 0) * pps + i],) + (0,) * 3

    nat, tr = _sample_row_specs(h, hd)
    in_specs = [nat, tr, nat, tr]
    in_specs += [pl.BlockSpec((None, h, hd, PAGE_SIZE), kpage(i)) for i in range(pps)]
    in_specs += [pl.BlockSpec((None, h, hd, PAGE_SIZE), vpage(i)) for i in range(pps)]
    out = pl.pallas_call(
        _moba_sample_kernel,
        grid_spec=pltpu.PrefetchScalarGridSpec(
            num_scalar_prefetch=1, grid=(s_n, 2 * nc), in_specs=in_specs,
            out_specs=pl.BlockSpec((None, 1, h * hd), lambda s, c, pt: (s, 0, 0)),
            scratch_shapes=[pltpu.VMEM((h, hd, PAGE_SIZE), F32), pltpu.VMEM((nc, h, nk), F32),
                            pltpu.VMEM((nc, h, nk), F32), pltpu.VMEM((h, 1), F32),
                            pltpu.VMEM((h, hd, PAGE_SIZE), F32)]),
        out_shape=jax.ShapeDtypeStruct((s_n, 1, h * hd), F32),
        compiler_params=_params("parallel", "arbitrary"),
        name="moba_sample",
    )(page_table.reshape(-1), q, jnp.swapaxes(q, 1, 2), k_new, jnp.swapaxes(v_new, 1, 2),
      *([cache_kt] * pps), *([cache_vt] * pps))
    return out.reshape(s_n, h * hd)


def _merge_kernel(h_ref, oa_ref, ob_ref, gl_ref, ga_ref, woa_ref, wob_ref, wout_ref, o_ref):
    d = h_ref.shape[1]
    ya = _dot(oa_ref[...].astype(BF16), woa_ref[...])
    yb = _dot(ob_ref[...].astype(BF16), wob_ref[...])
    gl = gl_ref[...]
    merged = jax.nn.sigmoid(gl[:, :d]) * ya + jax.nn.sigmoid(gl[:, d:]) * yb
    o_ref[...] = h_ref[...] + ga_ref[...] * _dot(merged.astype(BF16), wout_ref[...])


def _merge(h, oa, ob, gl, mod, woa, wob, wout, tokens_per_seq):
    t, d = h.shape
    tm = min(t, 512)
    tps = max(tokens_per_seq // tm, 1)
    tok = lambda i: (i, 0)
    const = lambda i: (0, 0)
    return pl.pallas_call(
        _merge_kernel,
        grid=(t // tm,),
        in_specs=[pl.BlockSpec((tm, d), tok), pl.BlockSpec((tm, oa.shape[1]), tok),
                  pl.BlockSpec((tm, ob.shape[1]), tok), pl.BlockSpec((tm, gl.shape[1]), tok),
                  _mod_spec(mod, 5, tm, tps),
                  pl.BlockSpec(woa.shape, const), pl.BlockSpec(wob.shape, const), pl.BlockSpec(wout.shape, const)],
        out_specs=pl.BlockSpec((tm, d), tok),
        out_shape=jax.ShapeDtypeStruct((t, d), F32),
        compiler_params=_params("parallel"),
        name="merge",
    )(h, oa, ob, gl, mod, woa, wob, wout)


def _rope_tables(pos):
    half = HEAD_DIM // 2
    inv = ROPE_THETA ** (-jnp.arange(half, dtype=F32) / half)
    ang = pos.astype(F32)[:, None] * inv[None, :]
    cos, sin = jnp.cos(ang), jnp.sin(ang)
    reps = LANES // HEAD_DIM
    return (jnp.tile(jnp.concatenate([cos, cos], axis=1), (1, reps)),
            jnp.tile(jnp.concatenate([-sin, sin], axis=1), (1, reps)))


def kernel(x_prompt, x_sample, cache_moba_k, cache_moba_v, cache_fox_k, cache_fox_v, cache_fox_logf, page_table,
           c_prompt, c_sample, w_ada, b_ada, g_ff1, w_ff1_in, w_ff1_out, g_mix, w_mix, b_forget, w_o_moba,
           w_o_fox, w_out, g_ff2, w_ff2_in, w_ff2_out, g_final):
    bsz, l, d = x_prompt.shape
    s_n, dec_seq, _ = x_sample.shape
    assert dec_seq == 1
    depth = w_ada.shape[0]
    h_moba, h_fox = cache_moba_k.shape[3], cache_fox_k.shape[3]
    w_moba, w_fox = h_moba * HEAD_DIM, h_fox * HEAD_DIM
    past_len = page_table.shape[1] * PAGE_SIZE
    n_qkv = 3 * w_moba + 3 * w_fox
    n_blocks = l // MOBA_BLOCK
    assert l % MOBA_BLOCK == 0 and MOBA_LANE0 + n_blocks <= LANES

    cos_p, sin_p = _rope_tables(jnp.arange(l, dtype=jnp.int32))
    cos_s, sin_s = _rope_tables(jnp.full((1,), past_len, jnp.int32))
    pad_rows = (-(s_n + bsz)) % 8
    c_all = jnp.concatenate([c_sample, c_prompt, jnp.zeros((pad_rows, d), F32)], axis=0)
    block_lanes = (jnp.arange(LANES, dtype=jnp.int32)[None, :] - MOBA_LANE0
                   == jnp.arange(l, dtype=jnp.int32)[:, None] // MOBA_BLOCK).astype(BF16)

    hp = x_prompt.reshape(bsz * l, d)
    hs = x_sample.reshape(s_n, d)
    st_p = [[] for _ in range(5)]
    st_s = [[] for _ in range(5)]
    for layer in range(depth):
        mod = _modulation(c_all, w_ada[layer], b_ada[layer])
        mod_s = mod[:, :s_n]
        mod_p = mod[:, s_n:s_n + bsz].reshape(N_MOD, bsz, 1, d)
        w1i, w1o = w_ff1_in[layer].astype(BF16), w_ff1_out[layer].astype(BF16)
        w2i, w2o = w_ff2_in[layer].astype(BF16), w_ff2_out[layer].astype(BF16)
        wm = w_mix[layer]
        wqkv = wm[:, :n_qkv].astype(BF16)
        wf = jnp.pad(wm[:, n_qkv:n_qkv + h_fox], ((0, 0), (0, LANES - h_fox))).astype(BF16)
        wg = wm[:, n_qkv + h_fox:].astype(BF16)
        bfor = jnp.pad(b_forget[layer], (0, LANES - h_fox)).reshape(1, LANES)
        woa, wob, wout = w_o_moba[layer].astype(BF16), w_o_fox[layer].astype(BF16), w_out[layer].astype(BF16)
        last = layer == depth - 1
        mix_kw = dict(w_moba=w_moba, w_fox=w_fox, n_fox=h_fox)

        hp = _ffn(hp, mod_p, 0, g_ff1[layer], w1i, w1o, l)
        (kat, vat, kbt, vbt, lft, gl, ka16, kb16, qat16, vat16, qbt16, vbt16, kbias, cumt, km) = _mix(
            hp, mod_p, g_mix[layer], cos_p, sin_p, wqkv, wf, bfor, wg, seq_shape=(bsz, l), **mix_kw)
        km = jnp.pad(km.reshape(bsz, n_blocks, w_moba), ((0, 0), (MOBA_LANE0, LANES - MOBA_LANE0 - n_blocks), (0, 0)))
        oa = _prompt_attn(qat16, ka16.reshape(bsz, l, w_moba), vat16, block_lanes, km, moba=True, n_fox=h_fox)
        ob = _prompt_attn(qbt16, kb16.reshape(bsz, l, w_fox), vbt16, kbias.reshape(bsz, l, LANES), cumt,
                          moba=False, n_fox=h_fox)
        hp = _merge(hp, oa.reshape(bsz * l, w_moba), ob.reshape(bsz * l, w_fox), gl, mod_p, woa, wob, wout, l)
        hp = _ffn(hp, mod_p, 6, g_ff2[layer], w2i, w2o, l, g_final=g_final if last else None)
        heads = lambda a, h: jnp.transpose(a.reshape(bsz, h, HEAD_DIM, l), (0, 3, 1, 2))
        for i, a in enumerate((heads(kat, h_moba), heads(vat, h_moba), heads(kbt, h_fox), heads(vbt, h_fox),
                               jnp.swapaxes(lft, 1, 2))):
            st_p[i].append(a)

        hs = _ffn(hs, mod_s, 0, g_ff1[layer], w1i, w1o, 1)
        ka, va, kb, vb, lf, gl, qa, qb = _mix(hs, mod_s, g_mix[layer], cos_s, sin_s, wqkv, wf, bfor, wg,
                                              seq_shape=None, **mix_kw)
        ka, va, qa = (a.reshape(s_n, h_moba, HEAD_DIM) for a in (ka, va, qa))
        kb, vb, qb = (a.reshape(s_n, h_fox, HEAD_DIM) for a in (kb, vb, qb))
        pages_t = lambda a: jnp.transpose(a[layer], (0, 2, 3, 1))
        oa = _moba_sample(page_table, qa, ka, va, pages_t(cache_moba_k), pages_t(cache_moba_v))
        ob = _fox_sample(page_table, qb, kb, vb, lf, pages_t(cache_fox_k), pages_t(cache_fox_v),
                         jnp.swapaxes(cache_fox_logf[layer], 1, 2))
        hs = _merge(hs, oa, ob, gl, mod_s, woa, wob, wout, 1)
        hs = _ffn(hs, mod_s, 6, g_ff2[layer], w2i, w2o, 1, g_final=g_final if last else None)
        for i, a in enumerate((ka.reshape(s_n, 1, h_moba, HEAD_DIM), va.reshape(s_n, 1, h_moba, HEAD_DIM),
                               kb.reshape(s_n, 1, h_fox, HEAD_DIM), vb.reshape(s_n, 1, h_fox, HEAD_DIM),
                               lf.reshape(s_n, 1, h_fox))):
            st_s[i].append(a)

    return (hp.reshape(bsz, l, d), hs.reshape(s_n, 1, d),
            *(jnp.stack(a) for a in st_p), *(jnp.stack(a) for a in st_s))
```

```python
import functools
import math

import jax
import jax.numpy as jnp
from jax import lax
from jax.experimental import pallas as pl
from jax.experimental.pallas import tpu as pltpu

HEAD_DIM = 64
MOBA_BLOCK = 256
MOBA_TOPK = 3
PAGE_SIZE = 128
ROPE_THETA = 10000.0
NORM_EPS = 1e-6
NEG_INF = -1e30
N_MOD = 9
LANES = 128
HEADS_PER_GROUP = LANES // HEAD_DIM
VMEM_LIMIT = 56 * 1024 * 1024

F32 = jnp.float32
BF16 = jnp.bfloat16
SCALE = 1.0 / math.sqrt(HEAD_DIM)

N_PIECES = 3
MOBA_LANE0 = 32
Q_SUBTILES = 2


def _params(*sem):
    return pltpu.CompilerParams(dimension_semantics=sem, vmem_limit_bytes=VMEM_LIMIT)


def _dot(a, b):
    return jnp.dot(a, b, preferred_element_type=F32)


def _dot_nt(a, b):
    return lax.dot_general(a, b, (((1,), (1,)), ((), ())), preferred_element_type=F32)


def _rms(x, g):
    return x * lax.rsqrt(jnp.mean(x * x, axis=-1, keepdims=True) + NORM_EPS) * g


def _silu(x):
    return x * jax.nn.sigmoid(x)


def _split_bf16(x):
    hi = x.astype(BF16).astype(F32)
    r = x - hi
    mid = r.astype(BF16).astype(F32)
    lo = (r - mid).astype(BF16).astype(F32)
    return hi, mid, lo


def _mod_kernel(c_ref, w_ref, b_ref, o_ref):
    a = _silu(c_ref[...]).astype(BF16)
    o_ref[...] = _dot(a, w_ref[...].astype(BF16)) + b_ref[...]


def _modulation(c, w_ada, b_ada):
    r, d = c.shape
    return pl.pallas_call(
        _mod_kernel,
        grid=(N_MOD,),
        in_specs=[pl.BlockSpec((r, d), lambda j: (0, 0)),
                  pl.BlockSpec((d, d), lambda j: (0, j)),
                  pl.BlockSpec((1, d), lambda j: (0, j))],
        out_specs=pl.BlockSpec((None, r, d), lambda j: (j, 0, 0)),
        out_shape=jax.ShapeDtypeStruct((N_MOD, r, d), F32),
        compiler_params=_params("arbitrary"),
        name="mod",
    )(c, w_ada, b_ada.reshape(1, -1))


def _mod_spec(mod, k, tm, tiles_per_seq):
    d = mod.shape[-1]
    if mod.ndim == 4:
        return pl.BlockSpec((None, None, 1, d), lambda i, *_: (k, i // tiles_per_seq, 0, 0))
    return pl.BlockSpec((None, tm, d), lambda i, *_: (k, i, 0))


def _ffn_kernel(x_ref, sh_ref, sc_ref, ga_ref, g_ref, wa_ref, wb_ref, wo_ref, *rest, final_norm):
    if final_norm:
        gf_ref, o_ref, n_scr, acc_scr = rest
    else:
        o_ref, n_scr, acc_scr = rest
    c = pl.program_id(1)

    @pl.when(c == 0)
    def _():
        n = _rms(x_ref[...], g_ref[...]) * (1 + sc_ref[...]) + sh_ref[...]
        n_scr[...] = n.astype(BF16)
        acc_scr[...] = jnp.zeros_like(acc_scr)

    n = n_scr[...]
    a = _dot(n, wa_ref[...])
    b = _dot(n, wb_ref[...])
    acc_scr[...] += _dot((_silu(a) * b).astype(BF16), wo_ref[...])

    @pl.when(c == pl.num_programs(1) - 1)
    def _():
        h = x_ref[...] + 0.5 * ga_ref[...] * acc_scr[...]
        if final_norm:
            h = _rms(h, gf_ref[...])
        o_ref[...] = h


def _ffn_tiles(t, d_ff):
    tm = min(t, 512)
    tf = d_ff // 2 if (d_ff // 2) % LANES == 0 else d_ff
    return tm, tf


def _ffn(x, mod, k0, g, w_in, w_out, tiles_per_seq_tokens, g_final=None):
    t, d = x.shape
    d_ff = w_out.shape[0]
    tm, tf = _ffn_tiles(t, d_ff)
    nc = d_ff // tf
    tps = max(tiles_per_seq_tokens // tm, 1)
    in_specs = [pl.BlockSpec((tm, d), lambda i, c: (i, 0)),
                _mod_spec(mod, k0, tm, tps), _mod_spec(mod, k0 + 1, tm, tps), _mod_spec(mod, k0 + 2, tm, tps),
                pl.BlockSpec((1, d), lambda i, c: (0, 0)),
                pl.BlockSpec((d, tf), lambda i, c: (0, c)),
                pl.BlockSpec((d, tf), lambda i, c: (0, c + nc)),
                pl.BlockSpec((tf, d), lambda i, c: (c, 0))]
    args = [x, mod, mod, mod, g.reshape(1, d), w_in, w_in, w_out]
    if g_final is not None:
        in_specs.append(pl.BlockSpec((1, d), lambda i, c: (0, 0)))
        args.append(g_final.reshape(1, d))
    return pl.pallas_call(
        functools.partial(_ffn_kernel, final_norm=g_final is not None),
        grid=(t // tm, nc),
        in_specs=in_specs,
        out_specs=pl.BlockSpec((tm, d), lambda i, c: (i, 0)),
        out_shape=jax.ShapeDtypeStruct((t, d), F32),
        scratch_shapes=[pltpu.VMEM((tm, d), BF16), pltpu.VMEM((tm, d), F32)],
        compiler_params=_params("parallel", "arbitrary"),
        name="ffn_final" if g_final is not None else "ffn",
    )(*args)


def _rope(x, cos, sin_signed):
    w = x.shape[1]
    reps = w // LANES
    cos = jnp.concatenate([cos] * reps, axis=1)
    sin_signed = jnp.concatenate([sin_signed] * reps, axis=1)
    lane = lax.broadcasted_iota(jnp.int32, x.shape, 1)
    first_half = (lane % HEAD_DIM) < (HEAD_DIM // 2)
    swapped = jnp.where(first_half, pltpu.roll(x, w - HEAD_DIM // 2, 1), pltpu.roll(x, HEAD_DIM // 2, 1))
    return x * cos + swapped * sin_signed


def _mix_kernel(h_ref, sh_ref, sc_ref, g_ref, cos_ref, sin_ref, wqkv_ref, wf_ref, bf_ref, wg_ref, *rest,
                w_moba, w_fox, n_fox, seq):
    if seq:
        (kat_ref, vat_ref, kbt_ref, vbt_ref, lft_ref, gl_ref, ka16_ref, kb16_ref, qat16_ref, vat16_ref, qbt16_ref,
         vbt16_ref, kbias_ref, cumt_ref, km_ref, carry_scr) = rest
    else:
        ka_ref, va_ref, kb_ref, vb_ref, lf_ref, gl_ref, qa_ref, qb_ref = rest
    n = (_rms(h_ref[...], g_ref[...]) * (1 + sc_ref[...]) + sh_ref[...]).astype(BF16)
    z = _dot(n, wqkv_ref[...])
    o = 0
    qa = _rope(z[:, o:o + w_moba], cos_ref[...], sin_ref[...]); o += w_moba
    ka = _rope(z[:, o:o + w_moba], cos_ref[...], sin_ref[...]); o += w_moba
    va = z[:, o:o + w_moba]; o += w_moba
    qb = z[:, o:o + w_fox]; o += w_fox
    kb = z[:, o:o + w_fox]; o += w_fox
    vb = z[:, o:o + w_fox]
    lf = jax.nn.log_sigmoid(_dot(n, wf_ref[...]) + bf_ref[...])
    gl_ref[...] = _dot(n, wg_ref[...])
    if not seq:
        ka_ref[...] = ka
        va_ref[...] = va
        kb_ref[...] = kb
        vb_ref[...] = vb
        lf_ref[...] = lf[:, :n_fox]
        qa_ref[...] = qa
        qb_ref[...] = qb
        return
    kat, vat, kbt, vbt = ka.T, va.T, kb.T, vb.T
    kat_ref[...] = kat
    vat_ref[...] = vat
    kbt_ref[...] = kbt
    vbt_ref[...] = vbt
    lft_ref[...] = lf.T[:n_fox, :]
    ka16_ref[...] = ka.astype(BF16)
    kb16_ref[...] = kb.astype(BF16)
    qat16_ref[...] = (qa * SCALE).T.astype(BF16)
    vat16_ref[...] = vat.astype(BF16)
    qbt16_ref[...] = (qb * SCALE).T.astype(BF16)
    vbt16_ref[...] = vbt.astype(BF16)
    tm = lf.shape[0]

    @pl.when(pl.program_id(1) == 0)
    def _():
        carry_scr[...] = jnp.zeros_like(carry_scr)

    row = lax.broadcasted_iota(jnp.int32, (tm, tm), 0)
    col = lax.broadcasted_iota(jnp.int32, (tm, tm), 1)
    tri = (row >= col).astype(F32)
    cum = jnp.dot(tri, lf, precision=lax.Precision.HIGHEST, preferred_element_type=F32) + carry_scr[...]
    carry_scr[...] = cum[tm - 1:tm, :]
    cumt_ref[...] = cum.T[:n_fox, :]
    hi, mid, lo = _split_bf16(cum)
    lane = lax.broadcasted_iota(jnp.int32, cum.shape, 1)
    kbias = jnp.where(lane < n_fox, -hi,
                      jnp.where(lane < 2 * n_fox, -pltpu.roll(mid, n_fox, 1),
                                jnp.where(lane < 3 * n_fox, -pltpu.roll(lo, 2 * n_fox, 1),
                                          jnp.where(lane < 3 * n_fox + N_PIECES, 1.0, 0.0))))
    kbias_ref[...] = kbias.astype(BF16)
    km_ref[...] = jnp.sum(ka, axis=0, keepdims=True) * (1.0 / MOBA_BLOCK)


def _mix(h, mod, g, cos, sin_signed, wqkv, wf, bf, wg, *, w_moba, w_fox, n_fox, seq_shape):
    t, d = h.shape
    seq = seq_shape is not None
    tm = min(t, MOBA_BLOCK)
    if seq:
        bsz, l = seq_shape
        nt = l // tm
        grid = (bsz, nt)
        tok = lambda b, i: (b * nt + i, 0)
        rope_spec = pl.BlockSpec((tm, LANES), lambda b, i: (i, 0))
        mod_specs = [pl.BlockSpec((None, None, 1, d), functools.partial(lambda b, i, k: (k, b, 0, 0), k=k))
                     for k in (3, 4)]
        sem = ("parallel", "arbitrary")
    else:
        grid = (t // tm,)
        tok = lambda i: (i, 0)
        rope_spec = pl.BlockSpec((1, LANES), lambda i: (0, 0))
        mod_specs = [pl.BlockSpec((None, tm, d), functools.partial(lambda i, k: (k, i, 0), k=k)) for k in (3, 4)]
        sem = ("parallel",)
    const = lambda *_: (0, 0)
    in_specs = [pl.BlockSpec((tm, d), tok), *mod_specs,
                pl.BlockSpec((1, d), const), rope_spec, rope_spec,
                pl.BlockSpec(wqkv.shape, const), pl.BlockSpec(wf.shape, const), pl.BlockSpec(bf.shape, const),
                pl.BlockSpec(wg.shape, const)]
    scratch = []
    if seq:
        assert tm == MOBA_BLOCK and 3 * n_fox + N_PIECES <= LANES
        tr = lambda b, i: (b, 0, i)
        out_specs = [pl.BlockSpec((None, w_moba, tm), tr), pl.BlockSpec((None, w_moba, tm), tr),
                     pl.BlockSpec((None, w_fox, tm), tr), pl.BlockSpec((None, w_fox, tm), tr),
                     pl.BlockSpec((None, n_fox, tm), tr), pl.BlockSpec((tm, wg.shape[1]), tok),
                     pl.BlockSpec((tm, w_moba), tok), pl.BlockSpec((tm, w_fox), tok),
                     pl.BlockSpec((None, w_moba, tm), tr), pl.BlockSpec((None, w_moba, tm), tr),
                     pl.BlockSpec((None, w_fox, tm), tr), pl.BlockSpec((None, w_fox, tm), tr),
                     pl.BlockSpec((tm, LANES), tok),
                     pl.BlockSpec((None, n_fox, tm), tr),
                     pl.BlockSpec((None, None, 1, w_moba), lambda b, i: (b, i, 0, 0))]
        out_shape = [jax.ShapeDtypeStruct((bsz, w_moba, l), F32), jax.ShapeDtypeStruct((bsz, w_moba, l), F32),
                     jax.ShapeDtypeStruct((bsz, w_fox, l), F32), jax.ShapeDtypeStruct((bsz, w_fox, l), F32),
                     jax.ShapeDtypeStruct((bsz, n_fox, l), F32), jax.ShapeDtypeStruct((t, wg.shape[1]), F32),
                     jax.ShapeDtypeStruct((t, w_moba), BF16), jax.ShapeDtypeStruct((t, w_fox), BF16),
                     jax.ShapeDtypeStruct((bsz, w_moba, l), BF16), jax.ShapeDtypeStruct((bsz, w_moba, l), BF16),
                     jax.ShapeDtypeStruct((bsz, w_fox, l), BF16), jax.ShapeDtypeStruct((bsz, w_fox, l), BF16),
                     jax.ShapeDtypeStruct((t, LANES), BF16),
                     jax.ShapeDtypeStruct((bsz, n_fox, l), F32),
                     jax.ShapeDtypeStruct((bsz, nt, 1, w_moba), F32)]
        scratch = [pltpu.VMEM((1, LANES), F32)]
    else:
        out_specs = [pl.BlockSpec((tm, w_moba), tok), pl.BlockSpec((tm, w_moba), tok),
                     pl.BlockSpec((tm, w_fox), tok), pl.BlockSpec((tm, w_fox), tok),
                     pl.BlockSpec((tm, n_fox), tok), pl.BlockSpec((tm, wg.shape[1]), tok),
                     pl.BlockSpec((tm, w_moba), tok), pl.BlockSpec((tm, w_fox), tok)]
        out_shape = [jax.ShapeDtypeStruct((t, w_moba), F32), jax.ShapeDtypeStruct((t, w_moba), F32),
                     jax.ShapeDtypeStruct((t, w_fox), F32), jax.ShapeDtypeStruct((t, w_fox), F32),
                     jax.ShapeDtypeStruct((t, n_fox), F32), jax.ShapeDtypeStruct((t, wg.shape[1]), F32),
                     jax.ShapeDtypeStruct((t, w_moba), F32), jax.ShapeDtypeStruct((t, w_fox), F32)]
    return pl.pallas_call(
        functools.partial(_mix_kernel, w_moba=w_moba, w_fox=w_fox, n_fox=n_fox, seq=seq),
        grid=grid, in_specs=in_specs, out_specs=out_specs, out_shape=out_shape, scratch_shapes=scratch,
        compiler_params=_params(*sem),
        name="mix_seq" if seq else "mix_tok",
    )(h, mod, mod, g.reshape(1, d), cos, sin_signed, wqkv, wf, bf, wg)


def _prompt_attn_kernel(qt_ref, k_ref, vt_ref, kbias_ref, side_ref, o_ref, *scratch, moba, n_fox):
    g = pl.program_id(1)
    i = pl.program_id(2)
    tq = MOBA_BLOCK
    chains = [(sub, head) for sub in range(Q_SUBTILES) for head in range(HEADS_PER_GROUP)]
    n_chain = len(chains)
    s_bufs = [scratch[b * n_chain:(b + 1) * n_chain] for b in range(2)]
    mx_bufs = [scratch[(2 + b) * n_chain:(3 + b) * n_chain] for b in range(2)]
    wrow = lax.broadcasted_iota(jnp.int32, (LANES, tq), 0)
    ones_rows = jnp.ones((16, 1), BF16)
    first_blk = i * Q_SUBTILES

    weights = []
    for sub, head in chains:
        own = first_blk + sub
        qt = qt_ref[:, sub * tq:(sub + 1) * tq]
        w_top = jnp.where(wrow // HEAD_DIM == head, qt, jnp.zeros_like(qt))
        if moba:
            km = side_ref[...].astype(BF16)
            kmh = jnp.where(lax.broadcasted_iota(jnp.int32, km.shape, 1) // HEAD_DIM == head, km, jnp.zeros_like(km))
            blk = (wrow - MOBA_LANE0).astype(F32)
            past = (wrow >= MOBA_LANE0) & (wrow < MOBA_LANE0 + own)
            gate = jnp.where(past, _dot(kmh, qt), NEG_INF)
            sel = jnp.zeros(gate.shape, jnp.bool_)
            for _ in range(MOBA_TOPK):
                best = jnp.max(gate, axis=0, keepdims=True)
                pick = jnp.min(jnp.where(gate == best, blk, float(LANES)), axis=0, keepdims=True)
                chosen = blk == pick
                sel = sel | chosen
                gate = jnp.where(chosen, -jnp.inf, gate)
            w_bot = jnp.where(past & jnp.logical_not(sel), NEG_INF, 0.0)
        else:
            hg = g * HEADS_PER_GROUP + head
            qc = side_ref[head:head + 1, pl.ds(pl.multiple_of(own * tq, tq), tq)]
            hi, mid, lo = _split_bf16(qc)
            w_bot = jnp.where((wrow < N_PIECES * n_fox) & (wrow % n_fox == hg), 1.0,
                              jnp.where(wrow == N_PIECES * n_fox, hi,
                                        jnp.where(wrow == N_PIECES * n_fox + 1, mid,
                                                  jnp.where(wrow == N_PIECES * n_fox + 2, lo, 0.0))))
        weights.append(jnp.concatenate([w_top, w_bot.astype(BF16)], axis=0))

    def keys(blk):
        off = pl.multiple_of(blk * tq, tq)
        return jnp.concatenate([k_ref[pl.ds(off, tq), :], kbias_ref[pl.ds(off, tq), :]], axis=1)

    def values(blk, head):
        vh = vt_ref[head * HEAD_DIM:(head + 1) * HEAD_DIM, pl.ds(pl.multiple_of(blk * tq, tq), tq)]
        return jnp.concatenate([vh, jnp.broadcast_to(ones_rows, (16, tq))], axis=0)

    def update(st, blk, head, m, acc):
        m_new = jnp.maximum(m, jnp.max(st, axis=0, keepdims=True))
        p = jnp.exp(st - m_new).astype(BF16)
        return m_new, jnp.exp(m - m_new) * acc + _dot(values(blk, head), p)

    krow = lax.broadcasted_iota(jnp.int32, (tq, tq), 0)
    qcol = lax.broadcasted_iota(jnp.int32, (tq, tq), 1)
    carry = []
    for c, (sub, head) in enumerate(chains):
        own = first_blk + sub
        st = jnp.where(krow <= qcol, _dot(keys(own), weights[c]), NEG_INF)
        m = jnp.max(st, axis=0, keepdims=True)
        p = jnp.exp(st - m).astype(BF16)
        acc = _dot(values(own, head), p)
        for earlier in range(sub):
            blk = first_blk + earlier
            m, acc = update(_dot(keys(blk), weights[c]), blk, head, m, acc)
        carry += [m, acc]

    def produce(blk, buf):
        kaug = keys(blk)
        for c in range(n_chain):
            st = _dot(kaug, weights[c])
            s_bufs[buf][c][...] = st
            mx_bufs[buf][c][...] = jnp.max(st.reshape(tq // 8, 8, tq), axis=0)

    def consume(blk, buf, carry):
        out = []
        for c, (sub, head) in enumerate(chains):
            m, acc = carry[2 * c], carry[2 * c + 1]
            m_new = jnp.maximum(m, jnp.max(mx_bufs[buf][c][...], axis=0, keepdims=True))
            p = jnp.exp(s_bufs[buf][c][...] - m_new).astype(BF16)
            out += [m_new, jnp.exp(m - m_new) * acc + _dot(values(blk, head), p)]
        return out

    def pair(t, carry):
        j = 2 * t
        produce(j + 1, 1)
        carry = consume(j, 0, carry)
        produce(j + 2, 0)
        return consume(j + 1, 1, carry)

    n_past = first_blk
    produce(0, 0)
    carry = lax.fori_loop(0, n_past // 2, pair, carry)
    if Q_SUBTILES % 2:
        carry = lax.cond(n_past % 2 == 1, lambda c: consume(n_past - 1, 0, c), lambda c: c, carry)

    for sub in range(Q_SUBTILES):
        outs = []
        for head in range(HEADS_PER_GROUP):
            acc = carry[2 * (sub * HEADS_PER_GROUP + head) + 1]
            outs.append(acc[:HEAD_DIM] / acc[HEAD_DIM:HEAD_DIM + 1])
        o_ref[sub * tq:(sub + 1) * tq, :] = jnp.concatenate(outs, axis=0).T.astype(o_ref.dtype)


def _prompt_attn(qt, k, vt, kbias, side, *, moba, n_fox):
    bsz, w, l = qt.shape
    tq = Q_SUBTILES * MOBA_BLOCK
    assert l % tq == 0
    groups = w // LANES
    n_chain = Q_SUBTILES * HEADS_PER_GROUP
    if moba:
        kbias_spec = pl.BlockSpec((l, LANES), lambda b, g, i: (0, 0))
        side_spec = pl.BlockSpec((None, LANES, LANES), lambda b, g, i: (b, 0, g))
    else:
        kbias_spec = pl.BlockSpec((None, l, LANES), lambda b, g, i: (b, 0, 0))
        side = side.reshape(bsz, groups, HEADS_PER_GROUP, l)
        side_spec = pl.BlockSpec((None, None, HEADS_PER_GROUP, l), lambda b, g, i: (b, g, 0, 0))
    return pl.pallas_call(
        functools.partial(_prompt_attn_kernel, moba=moba, n_fox=n_fox),
        grid=(bsz, groups, l // tq),
        in_specs=[pl.BlockSpec((None, LANES, tq), lambda b, g, i: (b, g, i)),
                  pl.BlockSpec((None, l, LANES), lambda b, g, i: (b, 0, g)),
                  pl.BlockSpec((None, LANES, l), lambda b, g, i: (b, g, 0)),
                  kbias_spec, side_spec],
        out_specs=pl.BlockSpec((None, tq, LANES), lambda b, g, i: (b, i, g)),
        out_shape=jax.ShapeDtypeStruct((bsz, l, w), BF16),
        scratch_shapes=([pltpu.VMEM((MOBA_BLOCK, MOBA_BLOCK), F32)] * (2 * n_chain)
                        + [pltpu.VMEM((8, MOBA_BLOCK), F32)] * (2 * n_chain)),
        compiler_params=_params("parallel", "parallel", "arbitrary"),
        name="moba_prompt" if moba else "fox_prompt",
    )(qt, k, vt, kbias, side)


PAGES_PER_STEP = 32


def _lane_bcast(col, n):
    return jnp.broadcast_to(col, (col.shape[0], n))


def _page_scores(qb_scr, k_refs):
    rows = []
    for head in range(qb_scr.shape[0]):
        qh = qb_scr[head]
        rows.append(jnp.concatenate([jnp.sum(r[head] * qh, axis=0, keepdims=True) for r in k_refs], axis=1))
    return jnp.concatenate(rows, axis=0)


def _accumulate_values(acc_scr, v_refs, p, alpha):
    page = acc_scr.shape[2]
    for head in range(acc_scr.shape[0]):
        a = acc_scr[head]
        if alpha is not None:
            a = a * alpha[head:head + 1, :]
        for i, r in enumerate(v_refs):
            a = a + r[head] * p[head:head + 1, i * page:(i + 1) * page]
        acc_scr[head] = a


def _finish_rows(acc_scr, l):
    h, hd, page = acc_scr.shape
    parts = [acc_scr[head] / l[head:head + 1, :] for head in range(h)]
    return jnp.sum(jnp.concatenate(parts, axis=0).T, axis=0, keepdims=True)


def _new_token_values(vnt_ref, acc_scr, weight):
    h, hd, page = acc_scr.shape
    lane0 = lax.broadcasted_iota(jnp.int32, (hd, page), 1) == 0
    for head in range(h):
        col = vnt_ref[:, head:head + 1]
        if weight is not None:
            col = col * weight[head:head + 1, :]
        acc_scr[head] = jnp.where(lane0, _lane_bcast(col, page), 0.0)


def _fox_sample_kernel(pt_ref, q_ref, qt_ref, kn_ref, vnt_ref, lfn_ref, *rest):
    pps = PAGES_PER_STEP
    k_refs, v_refs, lf_refs = rest[:pps], rest[pps:2 * pps], rest[2 * pps:3 * pps]
    o_ref, qb_scr, m_scr, l_scr, acc_scr, carry_scr = rest[3 * pps:]
    c = pl.program_id(1)
    page = acc_scr.shape[2]

    @pl.when(c == 0)
    def _():
        for head in range(qb_scr.shape[0]):
            qb_scr[head] = _lane_bcast(qt_ref[:, head:head + 1] * SCALE, page)
        m_scr[...] = jnp.sum(q_ref[...] * SCALE * kn_ref[...], axis=1, keepdims=True)
        l_scr[...] = jnp.ones_like(l_scr)
        _new_token_values(vnt_ref, acc_scr, None)
        carry_scr[...] = lfn_ref[...]

    lf = jnp.concatenate([r[...] for r in lf_refs], axis=1)
    nk = lf.shape[1]
    lane = lax.broadcasted_iota(jnp.int32, lf.shape, 1)
    sfx = jnp.where(lane < nk - 1, pltpu.roll(lf, nk - 1, 1), 0.0)
    d = 1
    while d < nk:
        sfx = sfx + jnp.where(lane < nk - d, pltpu.roll(sfx, nk - d, 1), 0.0)
        d *= 2
    carry = carry_scr[...]
    s = _page_scores(qb_scr, k_refs) + (carry + sfx)
    carry_scr[...] = carry + jnp.sum(lf, axis=1, keepdims=True)
    m = m_scr[...]
    m_new = jnp.maximum(m, jnp.max(s, axis=1, keepdims=True))
    alpha = jnp.exp(m - m_new)
    p = jnp.exp(s - m_new)
    m_scr[...] = m_new
    l_scr[...] = alpha * l_scr[...] + jnp.sum(p, axis=1, keepdims=True)
    _accumulate_values(acc_scr, v_refs, p, alpha)

    @pl.when(c == pl.num_programs(1) - 1)
    def _():
        o_ref[...] = _finish_rows(acc_scr, l_scr[...])


def _sample_row_specs(h, hd):
    nat = pl.BlockSpec((None, h, hd), lambda s, c, pt: (s, 0, 0))
    tr = pl.BlockSpec((None, hd, h), lambda s, c, pt: (s, 0, 0))
    return nat, tr


def _fox_sample(page_table, q, k_new, v_new, lf_new, cache_kt, cache_vt, cache_lft):
    s_n, h, hd = q.shape
    n_pages = page_table.shape[1]
    pps = PAGES_PER_STEP
    assert n_pages % pps == 0
    nc = n_pages // pps

    def page(i, nd):
        return lambda s, c, pt: (pt[s * n_pages + (n_pages - (c + 1) * pps + i)],) + (0,) * nd

    nat, tr = _sample_row_specs(h, hd)
    in_specs = [nat, tr, nat, tr, pl.BlockSpec((None, h, 1), lambda s, c, pt: (s, 0, 0))]
    in_specs += [pl.BlockSpec((None, h, hd, PAGE_SIZE), page(i, 3)) for i in range(pps)]
    in_specs += [pl.BlockSpec((None, h, hd, PAGE_SIZE), page(i, 3)) for i in range(pps)]
    in_specs += [pl.BlockSpec((None, h, PAGE_SIZE), page(i, 2)) for i in range(pps)]
    out = pl.pallas_call(
        _fox_sample_kernel,
        grid_spec=pltpu.PrefetchScalarGridSpec(
            num_scalar_prefetch=1, grid=(s_n, nc), in_specs=in_specs,
            out_specs=pl.BlockSpec((None, 1, h * hd), lambda s, c, pt: (s, 0, 0)),
            scratch_shapes=[pltpu.VMEM((h, hd, PAGE_SIZE), F32), pltpu.VMEM((h, 1), F32), pltpu.VMEM((h, 1), F32),
                            pltpu.VMEM((h, hd, PAGE_SIZE), F32), pltpu.VMEM((h, 1), F32)]),
        out_shape=jax.ShapeDtypeStruct((s_n, 1, h * hd), F32),
        compiler_params=_params("parallel", "arbitrary"),
        name="fox_sample",
    )(page_table.reshape(-1), q, jnp.swapaxes(q, 1, 2), k_new, jnp.swapaxes(v_new, 1, 2),
      lf_new.reshape(s_n, h, 1), *([cache_kt] * pps), *([cache_vt] * pps), *([cache_lft] * pps))
    return out.reshape(s_n, h * hd)


def _moba_sample_kernel(pt_ref, q_ref, qt_ref, kn_ref, vnt_ref, *rest):
    pps = PAGES_PER_STEP
    k_refs, v_refs = rest[:pps], rest[pps:2 * pps]
    o_ref, qb_scr, s_scr, p_scr, l_scr, acc_scr = rest[2 * pps:]
    c = pl.program_id(1)
    nc = s_scr.shape[0]
    nk = s_scr.shape[2]
    page = acc_scr.shape[2]

    @pl.when(c == 0)
    def _():
        for head in range(qb_scr.shape[0]):
            qb_scr[head] = _lane_bcast(qt_ref[:, head:head + 1], page)

    @pl.when(c < nc)
    def _():
        s_scr[c] = _page_scores(qb_scr, k_refs)

    @pl.when(c == nc - 1)
    def _():
        bpc = nk // MOBA_BLOCK
        gates = []
        for cc in range(nc):
            sc = s_scr[cc]
            for b in range(bpc):
                seg = sc[:, b * MOBA_BLOCK:(b + 1) * MOBA_BLOCK]
                gates.append(jnp.broadcast_to(jnp.sum(seg, axis=1, keepdims=True) * (1.0 / MOBA_BLOCK),
                                              seg.shape))
        gate = jnp.concatenate(gates, axis=1)
        blk = (lax.broadcasted_iota(jnp.int32, gate.shape, 1) // MOBA_BLOCK).astype(F32)
        n_blk = float(gate.shape[1] // MOBA_BLOCK)
        sel = jnp.zeros(gate.shape, jnp.bool_)
        for _ in range(MOBA_TOPK):
            best = jnp.max(gate, axis=1, keepdims=True)
            pick = jnp.min(jnp.where(gate == best, blk, n_blk), axis=1, keepdims=True)
            chosen = blk == pick
            sel = sel | chosen
            gate = jnp.where(chosen, -jnp.inf, gate)
        s_self = jnp.sum(q_ref[...] * kn_ref[...], axis=1, keepdims=True) * SCALE
        s_all = jnp.concatenate([s_scr[cc] for cc in range(nc)], axis=1) * SCALE
        s_all = jnp.where(sel, s_all, NEG_INF)
        m = jnp.maximum(jnp.max(s_all, axis=1, keepdims=True), s_self)
        p = jnp.exp(s_all - m)
        p_self = jnp.exp(s_self - m)
        l_scr[...] = jnp.sum(p, axis=1, keepdims=True) + p_self
        _new_token_values(vnt_ref, acc_scr, p_self)
        for cc in range(nc):
            p_scr[cc] = p[:, cc * nk:(cc + 1) * nk]

    @pl.when(c >= nc)
    def _():
        _accumulate_values(acc_scr, v_refs, p_scr[c - nc], None)

    @pl.when(c == 2 * nc - 1)
    def _():
        o_ref[...] = _finish_rows(acc_scr, l_scr[...])


def _moba_sample(page_table, q, k_new, v_new, cache_kt, cache_vt):
    s_n, h, hd = q.shape
    n_pages = page_table.shape[1]
    pps = PAGES_PER_STEP
    assert n_pages % pps == 0
    nc = n_pages // pps
    nk = pps * PAGE_SIZE
    assert nk % MOBA_BLOCK == 0 and n_pages * PAGE_SIZE // MOBA_BLOCK >= MOBA_TOPK

    def kpage(i):
        return lambda s, c, pt: (pt[s * n_pages + jnp.minimum(c, nc - 1) * pps + i],) + (0,) * 3

    def vpage(i):
        return lambda s, c, pt: (pt[s * n_pages + jnp.maximum(c - nc, 0) * pps + i],) + (0,) * 3

    nat, tr = _sample_row_specs(h, hd)
    in_specs = [nat, tr, nat, tr]
    in_specs += [pl.BlockSpec((None, h, hd, PAGE_SIZE), kpage(i)) for i in range(pps)]
    in_specs += [pl.BlockSpec((None, h, hd, PAGE_SIZE), vpage(i)) for i in range(pps)]
    out = pl.pallas_call(
        _moba_sample_kernel,
        grid_spec=pltpu.PrefetchScalarGridSpec(
            num_scalar_prefetch=1, grid=(s_n, 2 * nc), in_specs=in_specs,
            out_specs=pl.BlockSpec((None, 1, h * hd), lambda s, c, pt: (s, 0, 0)),
            scratch_shapes=[pltpu.VMEM((h, hd, PAGE_SIZE), F32), pltpu.VMEM((nc, h, nk), F32),
                            pltpu.VMEM((nc, h, nk), F32), pltpu.VMEM((h, 1), F32),
                            pltpu.VMEM((h, hd, PAGE_SIZE), F32)]),
        out_shape=jax.ShapeDtypeStruct((s_n, 1, h * hd), F32),
        compiler_params=_params("parallel", "arbitrary"),
        name="moba_sample",
    )(page_table.reshape(-1), q, jnp.swapaxes(q, 1, 2), k_new, jnp.swapaxes(v_new, 1, 2),
      *([cache_kt] * pps), *([cache_vt] * pps))
    return out.reshape(s_n, h * hd)


def _merge_kernel(h_ref, oa_ref, ob_ref, gl_ref, ga_ref, woa_ref, wob_ref, wout_ref, o_ref):
    d = h_ref.shape[1]
    ya = _dot(oa_ref[...].astype(BF16), woa_ref[...])
    yb = _dot(ob_ref[...].astype(BF16), wob_ref[...])
    gl = gl_ref[...]
    merged = jax.nn.sigmoid(gl[:, :d]) * ya + jax.nn.sigmoid(gl[:, d:]) * yb
    o_ref[...] = h_ref[...] + ga_ref[...] * _dot(merged.astype(BF16), wout_ref[...])


def _merge(h, oa, ob, gl, mod, woa, wob, wout, tokens_per_seq):
    t, d = h.shape
    tm = min(t, 512)
    tps = max(tokens_per_seq // tm, 1)
    tok = lambda i: (i, 0)
    const = lambda i: (0, 0)
    return pl.pallas_call(
        _merge_kernel,
        grid=(t // tm,),
        in_specs=[pl.BlockSpec((tm, d), tok), pl.BlockSpec((tm, oa.shape[1]), tok),
                  pl.BlockSpec((tm, ob.shape[1]), tok), pl.BlockSpec((tm, gl.shape[1]), tok),
                  _mod_spec(mod, 5, tm, tps),
                  pl.BlockSpec(woa.shape, const), pl.BlockSpec(wob.shape, const), pl.BlockSpec(wout.shape, const)],
        out_specs=pl.BlockSpec((tm, d), tok),
        out_shape=jax.ShapeDtypeStruct((t, d), F32),
        compiler_params=_params("parallel"),
        name="merge",
    )(h, oa, ob, gl, mod, woa, wob, wout)


def _rope_tables(pos):
    half = HEAD_DIM // 2
    inv = ROPE_THETA ** (-jnp.arange(half, dtype=F32) / half)
    ang = pos.astype(F32)[:, None] * inv[None, :]
    cos, sin = jnp.cos(ang), jnp.sin(ang)
    reps = LANES // HEAD_DIM
    return (jnp.tile(jnp.concatenate([cos, cos], axis=1), (1, reps)),
            jnp.tile(jnp.concatenate([-sin, sin], axis=1), (1, reps)))


def kernel(x_prompt, x_sample, cache_moba_k, cache_moba_v, cache_fox_k, cache_fox_v, cache_fox_logf, page_table,
           c_prompt, c_sample, w_ada, b_ada, g_ff1, w_ff1_in, w_ff1_out, g_mix, w_mix, b_forget, w_o_moba,
           w_o_fox, w_out, g_ff2, w_ff2_in, w_ff2_out, g_final):
    bsz, l, d = x_prompt.shape
    s_n, dec_seq, _ = x_sample.shape
    assert dec_seq == 1
    depth = w_ada.shape[0]
    h_moba, h_fox = cache_moba_k.shape[3], cache_fox_k.shape[3]
    w_moba, w_fox = h_moba * HEAD_DIM, h_fox * HEAD_DIM
    past_len = page_table.shape[1] * PAGE_SIZE
    n_qkv = 3 * w_moba + 3 * w_fox
    n_blocks = l // MOBA_BLOCK
    assert l % MOBA_BLOCK == 0 and MOBA_LANE0 + n_blocks <= LANES

    cos_p, sin_p = _rope_tables(jnp.arange(l, dtype=jnp.int32))
    cos_s, sin_s = _rope_tables(jnp.full((1,), past_len, jnp.int32))
    pad_rows = (-(s_n + bsz)) % 8
    c_all = jnp.concatenate([c_sample, c_prompt, jnp.zeros((pad_rows, d), F32)], axis=0)
    block_lanes = (jnp.arange(LANES, dtype=jnp.int32)[None, :] - MOBA_LANE0
                   == jnp.arange(l, dtype=jnp.int32)[:, None] // MOBA_BLOCK).astype(BF16)

    hp = x_prompt.reshape(bsz * l, d)
    hs = x_sample.reshape(s_n, d)
    st_p = [[] for _ in range(5)]
    st_s = [[] for _ in range(5)]
    for layer in range(depth):
        mod = _modulation(c_all, w_ada[layer], b_ada[layer])
        mod_s = mod[:, :s_n]
        mod_p = mod[:, s_n:s_n + bsz].reshape(N_MOD, bsz, 1, d)
        w1i, w1o = w_ff1_in[layer].astype(BF16), w_ff1_out[layer].astype(BF16)
        w2i, w2o = w_ff2_in[layer].astype(BF16), w_ff2_out[layer].astype(BF16)
        wm = w_mix[layer]
        wqkv = wm[:, :n_qkv].astype(BF16)
        wf = jnp.pad(wm[:, n_qkv:n_qkv + h_fox], ((0, 0), (0, LANES - h_fox))).astype(BF16)
        wg = wm[:, n_qkv + h_fox:].astype(BF16)
        bfor = jnp.pad(b_forget[layer], (0, LANES - h_fox)).reshape(1, LANES)
        woa, wob, wout = w_o_moba[layer].astype(BF16), w_o_fox[layer].astype(BF16), w_out[layer].astype(BF16)
        last = layer == depth - 1
        mix_kw = dict(w_moba=w_moba, w_fox=w_fox, n_fox=h_fox)

        hp = _ffn(hp, mod_p, 0, g_ff1[layer], w1i, w1o, l)
        (kat, vat, kbt, vbt, lft, gl, ka16, kb16, qat16, vat16, qbt16, vbt16, kbias, cumt, km) = _mix(
            hp, mod_p, g_mix[layer], cos_p, sin_p, wqkv, wf, bfor, wg, seq_shape=(bsz, l), **mix_kw)
        km = jnp.pad(km.reshape(bsz, n_blocks, w_moba), ((0, 0), (MOBA_LANE0, LANES - MOBA_LANE0 - n_blocks), (0, 0)))
        oa = _prompt_attn(qat16, ka16.reshape(bsz, l, w_moba), vat16, block_lanes, km, moba=True, n_fox=h_fox)
        ob = _prompt_attn(qbt16, kb16.reshape(bsz, l, w_fox), vbt16, kbias.reshape(bsz, l, LANES), cumt,
                          moba=False, n_fox=h_fox)
        hp = _merge(hp, oa.reshape(bsz * l, w_moba), ob.reshape(bsz * l, w_fox), gl, mod_p, woa, wob, wout, l)
        hp = _ffn(hp, mod_p, 6, g_ff2[layer], w2i, w2o, l, g_final=g_final if last else None)
        heads = lambda a, h: jnp.transpose(a.reshape(bsz, h, HEAD_DIM, l), (0, 3, 1, 2))
        for i, a in enumerate((heads(kat, h_moba), heads(vat, h_moba), heads(kbt, h_fox), heads(vbt, h_fox),
                               jnp.swapaxes(lft, 1, 2))):
            st_p[i].append(a)

        hs = _ffn(hs, mod_s, 0, g_ff1[layer], w1i, w1o, 1)
        ka, va, kb, vb, lf, gl, qa, qb = _mix(hs, mod_s, g_mix[layer], cos_s, sin_s, wqkv, wf, bfor, wg,
                                              seq_shape=None, **mix_kw)
        ka, va, qa = (a.reshape(s_n, h_moba, HEAD_DIM) for a in (ka, va, qa))
        kb, vb, qb = (a.reshape(s_n, h_fox, HEAD_DIM) for a in (kb, vb, qb))
        pages_t = lambda a: jnp.transpose(a[layer], (0, 2, 3, 1))
        oa = _moba_sample(page_table, qa, ka, va, pages_t(cache_moba_k), pages_t(cache_moba_v))
        ob = _fox_sample(page_table, qb, kb, vb, lf, pages_t(cache_fox_k), pages_t(cache_fox_v),
                         jnp.swapaxes(cache_fox_logf[layer], 1, 2))
        hs = _merge(hs, oa, ob, gl, mod_s, woa, wob, wout, 1)
        hs = _ffn(hs, mod_s, 6, g_ff2[layer], w2i, w2o, 1, g_final=g_final if last else None)
        for i, a in enumerate((ka.reshape(s_n, 1, h_moba, HEAD_DIM), va.reshape(s_n, 1, h_moba, HEAD_DIM),
                               kb.reshape(s_n, 1, h_fox, HEAD_DIM), vb.reshape(s_n, 1, h_fox, HEAD_DIM),
                               lf.reshape(s_n, 1, h_fox))):
            st_s[i].append(a)

    return (hp.reshape(bsz, l, d), hs.reshape(s_n, 1, d),
            *(jnp.stack(a) for a in st_p), *(jnp.stack(a) for a in st_s))
```

```python
import functools
import math

import jax
import jax.numpy as jnp
from jax import lax
from jax.experimental import pallas as pl
from jax.experimental.pallas import tpu as pltpu

HEAD_DIM = 64
MOBA_BLOCK = 256
MOBA_TOPK = 3
PAGE_SIZE = 128
ROPE_THETA = 10000.0
NORM_EPS = 1e-6
NEG_INF = -1e30
N_MOD = 9
LANES = 128
HEADS_PER_GROUP = LANES // HEAD_DIM
VMEM_LIMIT = 56 * 1024 * 1024

F32 = jnp.float32
BF16 = jnp.bfloat16
SCALE = 1.0 / math.sqrt(HEAD_DIM)

N_PIECES = 3
MOBA_LANE0 = 32
Q_SUBTILES = 2
EXP_ZERO_GAP = 110.0
BOUND_SLACK_MUL = 1.02
BOUND_SLACK_ADD = 0.01


def _params(*sem):
    return pltpu.CompilerParams(dimension_semantics=sem, vmem_limit_bytes=VMEM_LIMIT)


def _dot(a, b):
    return jnp.dot(a, b, preferred_element_type=F32)


def _dot_nt(a, b):
    return lax.dot_general(a, b, (((1,), (1,)), ((), ())), preferred_element_type=F32)


def _rms(x, g):
    return x * lax.rsqrt(jnp.mean(x * x, axis=-1, keepdims=True) + NORM_EPS) * g


def _silu(x):
    return x * jax.nn.sigmoid(x)


def _split_bf16(x):
    hi = x.astype(BF16).astype(F32)
    r = x - hi
    mid = r.astype(BF16).astype(F32)
    lo = (r - mid).astype(BF16).astype(F32)
    return hi, mid, lo


def _mod_kernel(c_ref, w_ref, b_ref, o_ref):
    a = _silu(c_ref[...]).astype(BF16)
    o_ref[...] = _dot(a, w_ref[...].astype(BF16)) + b_ref[...]


def _modulation(c, w_ada, b_ada):
    r, d = c.shape
    return pl.pallas_call(
        _mod_kernel,
        grid=(N_MOD,),
        in_specs=[pl.BlockSpec((r, d), lambda j: (0, 0)),
                  pl.BlockSpec((d, d), lambda j: (0, j)),
                  pl.BlockSpec((1, d), lambda j: (0, j))],
        out_specs=pl.BlockSpec((None, r, d), lambda j: (j, 0, 0)),
        out_shape=jax.ShapeDtypeStruct((N_MOD, r, d), F32),
        compiler_params=_params("arbitrary"),
        name="mod",
    )(c, w_ada, b_ada.reshape(1, -1))


def _mod_spec(mod, k, tm, tiles_per_seq):
    d = mod.shape[-1]
    if mod.ndim == 4:
        return pl.BlockSpec((None, None, 1, d), lambda i, *_: (k, i // tiles_per_seq, 0, 0))
    return pl.BlockSpec((None, tm, d), lambda i, *_: (k, i, 0))


def _ffn_kernel(x_ref, sh_ref, sc_ref, ga_ref, g_ref, wa_ref, wb_ref, wo_ref, *rest, final_norm):
    if final_norm:
        gf_ref, o_ref, n_scr, acc_scr = rest
    else:
        o_ref, n_scr, acc_scr = rest
    c = pl.program_id(1)

    @pl.when(c == 0)
    def _():
        n = _rms(x_ref[...], g_ref[...]) * (1 + sc_ref[...]) + sh_ref[...]
        n_scr[...] = n.astype(BF16)
        acc_scr[...] = jnp.zeros_like(acc_scr)

    n = n_scr[...]
    a = _dot(n, wa_ref[...])
    b = _dot(n, wb_ref[...])
    acc_scr[...] += _dot((_silu(a) * b).astype(BF16), wo_ref[...])

    @pl.when(c == pl.num_programs(1) - 1)
    def _():
        h = x_ref[...] + 0.5 * ga_ref[...] * acc_scr[...]
        if final_norm:
            h = _rms(h, gf_ref[...])
        o_ref[...] = h


def _ffn_tiles(t, d_ff):
    tm = min(t, 512)
    tf = d_ff // 2 if (d_ff // 2) % LANES == 0 else d_ff
    return tm, tf


def _ffn(x, mod, k0, g, w_in, w_out, tiles_per_seq_tokens, g_final=None):
    t, d = x.shape
    d_ff = w_out.shape[0]
    tm, tf = _ffn_tiles(t, d_ff)
    nc = d_ff // tf
    tps = max(tiles_per_seq_tokens // tm, 1)
    in_specs = [pl.BlockSpec((tm, d), lambda i, c: (i, 0)),
                _mod_spec(mod, k0, tm, tps), _mod_spec(mod, k0 + 1, tm, tps), _mod_spec(mod, k0 + 2, tm, tps),
                pl.BlockSpec((1, d), lambda i, c: (0, 0)),
                pl.BlockSpec((d, tf), lambda i, c: (0, c)),
                pl.BlockSpec((d, tf), lambda i, c: (0, c + nc)),
                pl.BlockSpec((tf, d), lambda i, c: (c, 0))]
    args = [x, mod, mod, mod, g.reshape(1, d), w_in, w_in, w_out]
    if g_final is not None:
        in_specs.append(pl.BlockSpec((1, d), lambda i, c: (0, 0)))
        args.append(g_final.reshape(1, d))
    return pl.pallas_call(
        functools.partial(_ffn_kernel, final_norm=g_final is not None),
        grid=(t // tm, nc),
        in_specs=in_specs,
        out_specs=pl.BlockSpec((tm, d), lambda i, c: (i, 0)),
        out_shape=jax.ShapeDtypeStruct((t, d), F32),
        scratch_shapes=[pltpu.VMEM((tm, d), BF16), pltpu.VMEM((tm, d), F32)],
        compiler_params=_params("parallel", "arbitrary"),
        name="ffn_final" if g_final is not None else "ffn",
    )(*args)


def _rope(x, cos, sin_signed):
    w = x.shape[1]
    reps = w // LANES
    cos = jnp.concatenate([cos] * reps, axis=1)
    sin_signed = jnp.concatenate([sin_signed] * reps, axis=1)
    lane = lax.broadcasted_iota(jnp.int32, x.shape, 1)
    first_half = (lane % HEAD_DIM) < (HEAD_DIM // 2)
    swapped = jnp.where(first_half, pltpu.roll(x, w - HEAD_DIM // 2, 1), pltpu.roll(x, HEAD_DIM // 2, 1))
    return x * cos + swapped * sin_signed


def _mix_kernel(h_ref, sh_ref, sc_ref, g_ref, cos_ref, sin_ref, wqkv_ref, wf_ref, bf_ref, wg_ref, *rest,
                w_moba, w_fox, n_fox, seq):
    if seq:
        (kat_ref, vat_ref, kbt_ref, vbt_ref, lft_ref, gl_ref, ka16_ref, kb16_ref, qat16_ref, vat16_ref, qbt16_ref,
         vbt16_ref, kbias_ref, cumt_ref, km_ref, kstat_ref, carry_scr) = rest
    else:
        ka_ref, va_ref, kb_ref, vb_ref, lf_ref, gl_ref, qa_ref, qb_ref = rest
    n = (_rms(h_ref[...], g_ref[...]) * (1 + sc_ref[...]) + sh_ref[...]).astype(BF16)
    z = _dot(n, wqkv_ref[...])
    o = 0
    qa = _rope(z[:, o:o + w_moba], cos_ref[...], sin_ref[...]); o += w_moba
    ka = _rope(z[:, o:o + w_moba], cos_ref[...], sin_ref[...]); o += w_moba
    va = z[:, o:o + w_moba]; o += w_moba
    qb = z[:, o:o + w_fox]; o += w_fox
    kb = z[:, o:o + w_fox]; o += w_fox
    vb = z[:, o:o + w_fox]
    lf = jax.nn.log_sigmoid(_dot(n, wf_ref[...]) + bf_ref[...])
    gl_ref[...] = _dot(n, wg_ref[...])
    if not seq:
        ka_ref[...] = ka
        va_ref[...] = va
        kb_ref[...] = kb
        vb_ref[...] = vb
        lf_ref[...] = lf[:, :n_fox]
        qa_ref[...] = qa
        qb_ref[...] = qb
        return
    kat, vat, kbt, vbt = ka.T, va.T, kb.T, vb.T
    kat_ref[...] = kat
    vat_ref[...] = vat
    kbt_ref[...] = kbt
    vbt_ref[...] = vbt
    lft_ref[...] = lf.T[:n_fox, :]
    ka16_ref[...] = ka.astype(BF16)
    kb16_ref[...] = kb.astype(BF16)
    qat16_ref[...] = (qa * SCALE).T.astype(BF16)
    vat16_ref[...] = vat.astype(BF16)
    qbt16_ref[...] = (qb * SCALE).T.astype(BF16)
    vbt16_ref[...] = vbt.astype(BF16)
    tm = lf.shape[0]

    @pl.when(pl.program_id(1) == 0)
    def _():
        carry_scr[...] = jnp.zeros_like(carry_scr)

    row = lax.broadcasted_iota(jnp.int32, (tm, tm), 0)
    col = lax.broadcasted_iota(jnp.int32, (tm, tm), 1)
    tri = (row >= col).astype(F32)
    cum = jnp.dot(tri, lf, precision=lax.Precision.HIGHEST, preferred_element_type=F32) + carry_scr[...]
    carry_scr[...] = cum[tm - 1:tm, :]
    cumt_ref[...] = cum.T[:n_fox, :]
    hi, mid, lo = _split_bf16(cum)
    lane = lax.broadcasted_iota(jnp.int32, cum.shape, 1)
    kbias = jnp.where(lane < n_fox, -hi,
                      jnp.where(lane < 2 * n_fox, -pltpu.roll(mid, n_fox, 1),
                                jnp.where(lane < 3 * n_fox, -pltpu.roll(lo, 2 * n_fox, 1),
                                          jnp.where(lane < 3 * n_fox + N_PIECES, 1.0, 0.0))))
    kbias_ref[...] = kbias.astype(BF16)
    km_ref[...] = jnp.sum(ka, axis=0, keepdims=True) * (1.0 / MOBA_BLOCK)
    head_of = lax.broadcasted_iota(jnp.int32, (w_fox, LANES), 0) // HEAD_DIM
    seg = (head_of == lax.broadcasted_iota(jnp.int32, (w_fox, LANES), 1)).astype(BF16)
    norm2 = jnp.max(_dot((kb * kb).astype(BF16), seg), axis=0, keepdims=True)
    kstat_ref[...] = jnp.concatenate([norm2, jnp.min(cum, axis=0, keepdims=True)], axis=0)


def _mix(h, mod, g, cos, sin_signed, wqkv, wf, bf, wg, *, w_moba, w_fox, n_fox, seq_shape):
    t, d = h.shape
    seq = seq_shape is not None
    tm = min(t, MOBA_BLOCK)
    if seq:
        bsz, l = seq_shape
        nt = l // tm
        grid = (bsz, nt)
        tok = lambda b, i: (b * nt + i, 0)
        rope_spec = pl.BlockSpec((tm, LANES), lambda b, i: (i, 0))
        mod_specs = [pl.BlockSpec((None, None, 1, d), functools.partial(lambda b, i, k: (k, b, 0, 0), k=k))
                     for k in (3, 4)]
        sem = ("parallel", "arbitrary")
    else:
        grid = (t // tm,)
        tok = lambda i: (i, 0)
        rope_spec = pl.BlockSpec((1, LANES), lambda i: (0, 0))
        mod_specs = [pl.BlockSpec((None, tm, d), functools.partial(lambda i, k: (k, i, 0), k=k)) for k in (3, 4)]
        sem = ("parallel",)
    const = lambda *_: (0, 0)
    in_specs = [pl.BlockSpec((tm, d), tok), *mod_specs,
                pl.BlockSpec((1, d), const), rope_spec, rope_spec,
                pl.BlockSpec(wqkv.shape, const), pl.BlockSpec(wf.shape, const), pl.BlockSpec(bf.shape, const),
                pl.BlockSpec(wg.shape, const)]
    scratch = []
    if seq:
        assert tm == MOBA_BLOCK and 3 * n_fox + N_PIECES <= LANES
        tr = lambda b, i: (b, 0, i)
        out_specs = [pl.BlockSpec((None, w_moba, tm), tr), pl.BlockSpec((None, w_moba, tm), tr),
                     pl.BlockSpec((None, w_fox, tm), tr), pl.BlockSpec((None, w_fox, tm), tr),
                     pl.BlockSpec((None, n_fox, tm), tr), pl.BlockSpec((tm, wg.shape[1]), tok),
                     pl.BlockSpec((tm, w_moba), tok), pl.BlockSpec((tm, w_fox), tok),
                     pl.BlockSpec((None, w_moba, tm), tr), pl.BlockSpec((None, w_moba, tm), tr),
                     pl.BlockSpec((None, w_fox, tm), tr), pl.BlockSpec((None, w_fox, tm), tr),
                     pl.BlockSpec((tm, LANES), tok),
                     pl.BlockSpec((None, n_fox, tm), tr),
                     pl.BlockSpec((None, None, 1, w_moba), lambda b, i: (b, i, 0, 0)),
                     pl.BlockSpec((None, None, 2, LANES), lambda b, i: (b, i, 0, 0))]
        out_shape = [jax.ShapeDtypeStruct((bsz, w_moba, l), F32), jax.ShapeDtypeStruct((bsz, w_moba, l), F32),
                     jax.ShapeDtypeStruct((bsz, w_fox, l), F32), jax.ShapeDtypeStruct((bsz, w_fox, l), F32),
                     jax.ShapeDtypeStruct((bsz, n_fox, l), F32), jax.ShapeDtypeStruct((t, wg.shape[1]), F32),
                     jax.ShapeDtypeStruct((t, w_moba), BF16), jax.ShapeDtypeStruct((t, w_fox), BF16),
                     jax.ShapeDtypeStruct((bsz, w_moba, l), BF16), jax.ShapeDtypeStruct((bsz, w_moba, l), BF16),
                     jax.ShapeDtypeStruct((bsz, w_fox, l), BF16), jax.ShapeDtypeStruct((bsz, w_fox, l), BF16),
                     jax.ShapeDtypeStruct((t, LANES), BF16),
                     jax.ShapeDtypeStruct((bsz, n_fox, l), F32),
                     jax.ShapeDtypeStruct((bsz, nt, 1, w_moba), F32),
                     jax.ShapeDtypeStruct((bsz, nt, 2, LANES), F32)]
        scratch = [pltpu.VMEM((1, LANES), F32)]
    else:
        out_specs = [pl.BlockSpec((tm, w_moba), tok), pl.BlockSpec((tm, w_moba), tok),
                     pl.BlockSpec((tm, w_fox), tok), pl.BlockSpec((tm, w_fox), tok),
                     pl.BlockSpec((tm, n_fox), tok), pl.BlockSpec((tm, wg.shape[1]), tok),
                     pl.BlockSpec((tm, w_moba), tok), pl.BlockSpec((tm, w_fox), tok)]
        out_shape = [jax.ShapeDtypeStruct((t, w_moba), F32), jax.ShapeDtypeStruct((t, w_moba), F32),
                     jax.ShapeDtypeStruct((t, w_fox), F32), jax.ShapeDtypeStruct((t, w_fox), F32),
                     jax.ShapeDtypeStruct((t, n_fox), F32), jax.ShapeDtypeStruct((t, wg.shape[1]), F32),
                     jax.ShapeDtypeStruct((t, w_moba), F32), jax.ShapeDtypeStruct((t, w_fox), F32)]
    return pl.pallas_call(
        functools.partial(_mix_kernel, w_moba=w_moba, w_fox=w_fox, n_fox=n_fox, seq=seq),
        grid=grid, in_specs=in_specs, out_specs=out_specs, out_shape=out_shape, scratch_shapes=scratch,
        compiler_params=_params(*sem),
        name="mix_seq" if seq else "mix_tok",
    )(h, mod, mod, g.reshape(1, d), cos, sin_signed, wqkv, wf, bf, wg)


def _prompt_attn_kernel(qt_ref, k_ref, vt_ref, kbias_ref, side_ref, *rest, moba, n_fox):
    if moba:
        o_ref, *scratch = rest
    else:
        kstat_ref, o_ref, *scratch = rest
    g = pl.program_id(1)
    i = pl.program_id(2)
    tq = MOBA_BLOCK
    chains = [(sub, head) for sub in range(Q_SUBTILES) for head in range(HEADS_PER_GROUP)]
    n_chain = len(chains)
    s_bufs = [scratch[b * n_chain:(b + 1) * n_chain] for b in range(2)]
    mx_bufs = [scratch[(2 + b) * n_chain:(3 + b) * n_chain] for b in range(2)]
    wrow = lax.broadcasted_iota(jnp.int32, (LANES, tq), 0)
    ones_rows = jnp.ones((16, 1), BF16)
    first_blk = i * Q_SUBTILES

    weights = []
    q_stats = []
    for sub, head in chains:
        own = first_blk + sub
        qt = qt_ref[:, sub * tq:(sub + 1) * tq]
        w_top = jnp.where(wrow // HEAD_DIM == head, qt, jnp.zeros_like(qt))
        if moba:
            km = side_ref[...].astype(BF16)
            kmh = jnp.where(lax.broadcasted_iota(jnp.int32, km.shape, 1) // HEAD_DIM == head, km, jnp.zeros_like(km))
            blk = (wrow - MOBA_LANE0).astype(F32)
            past = (wrow >= MOBA_LANE0) & (wrow < MOBA_LANE0 + own)
            gate = jnp.where(past, _dot(kmh, qt), NEG_INF)
            sel = jnp.zeros(gate.shape, jnp.bool_)
            for _ in range(MOBA_TOPK):
                best = jnp.max(gate, axis=0, keepdims=True)
                pick = jnp.min(jnp.where(gate == best, blk, float(LANES)), axis=0, keepdims=True)
                chosen = blk == pick
                sel = sel | chosen
                gate = jnp.where(chosen, -jnp.inf, gate)
            w_bot = jnp.where(past & jnp.logical_not(sel), NEG_INF, 0.0)
        else:
            hg = g * HEADS_PER_GROUP + head
            qc = side_ref[head:head + 1, pl.ds(pl.multiple_of(own * tq, tq), tq)]
            hi, mid, lo = _split_bf16(qc)
            qf = w_top.astype(F32)
            q_stats.append((jnp.max(jnp.sum(qf * qf, axis=0, keepdims=True), axis=1, keepdims=True),
                            jnp.max(qc, axis=1, keepdims=True)))
            w_bot = jnp.where((wrow < N_PIECES * n_fox) & (wrow % n_fox == hg), 1.0,
                              jnp.where(wrow == N_PIECES * n_fox, hi,
                                        jnp.where(wrow == N_PIECES * n_fox + 1, mid,
                                                  jnp.where(wrow == N_PIECES * n_fox + 2, lo, 0.0))))
        weights.append(jnp.concatenate([w_top, w_bot.astype(BF16)], axis=0))

    def keys(blk):
        off = pl.multiple_of(blk * tq, tq)
        return jnp.concatenate([k_ref[pl.ds(off, tq), :], kbias_ref[pl.ds(off, tq), :]], axis=1)

    def values(blk, head):
        vh = vt_ref[head * HEAD_DIM:(head + 1) * HEAD_DIM, pl.ds(pl.multiple_of(blk * tq, tq), tq)]
        return jnp.concatenate([vh, jnp.broadcast_to(ones_rows, (16, tq))], axis=0)

    def update(st, blk, head, m, acc):
        m_new = jnp.maximum(m, jnp.max(st, axis=0, keepdims=True))
        p = jnp.exp(st - m_new).astype(BF16)
        return m_new, jnp.exp(m - m_new) * acc + _dot(values(blk, head), p)

    krow = lax.broadcasted_iota(jnp.int32, (tq, tq), 0)
    qcol = lax.broadcasted_iota(jnp.int32, (tq, tq), 1)
    carry = []
    for c, (sub, head) in enumerate(chains):
        own = first_blk + sub
        st = jnp.where(krow <= qcol, _dot(keys(own), weights[c]), NEG_INF)
        m = jnp.max(st, axis=0, keepdims=True)
        p = jnp.exp(st - m).astype(BF16)
        acc = _dot(values(own, head), p)
        for earlier in range(sub):
            blk = first_blk + earlier
            m, acc = update(_dot(keys(blk), weights[c]), blk, head, m, acc)
        carry += [m, acc]

    def produce(blk, buf):
        kaug = keys(blk)
        for c in range(n_chain):
            st = _dot(kaug, weights[c])
            s_bufs[buf][c][...] = st
            mx_bufs[buf][c][...] = jnp.max(st.reshape(tq // 8, 8, tq), axis=0)

    def consume(blk, buf, carry):
        out = []
        for c, (sub, head) in enumerate(chains):
            m, acc = carry[2 * c], carry[2 * c + 1]
            m_new = jnp.maximum(m, jnp.max(mx_bufs[buf][c][...], axis=0, keepdims=True))
            p = jnp.exp(s_bufs[buf][c][...] - m_new).astype(BF16)
            out += [m_new, jnp.exp(m - m_new) * acc + _dot(values(blk, head), p)]
        return out

    def pair(t, carry):
        j = 2 * t
        produce(j + 1, 1)
        carry = consume(j, 0, carry)
        produce(j + 2, 0)
        return consume(j + 1, 1, carry)

    n_past = first_blk
    first_pair = 0
    if not moba:
        blk_lane = lax.broadcasted_iota(jnp.int32, (1, LANES), 1).astype(F32)
        first_needed = jnp.full((1, 1), float(LANES), F32)
        for c, (sub, head) in enumerate(chains):
            qn2, qc_max = q_stats[c]
            kn2 = kstat_ref[2 * head:2 * head + 1, :]
            kc_min = kstat_ref[2 * head + 1:2 * head + 2, :]
            bound = jnp.sqrt(qn2 * kn2) * BOUND_SLACK_MUL + BOUND_SLACK_ADD + (qc_max - kc_min)
            m_low = jnp.min(carry[2 * c], axis=1, keepdims=True)
            needed = jnp.logical_not(bound < m_low - EXP_ZERO_GAP)
            first_needed = jnp.minimum(first_needed,
                                       jnp.min(jnp.where(needed, blk_lane, float(LANES)), axis=1, keepdims=True))
        first_pair = jnp.clip(jnp.min(first_needed).astype(jnp.int32), 0, n_past) // 2
    produce(2 * first_pair, 0)
    carry = lax.fori_loop(first_pair, n_past // 2, pair, carry)
    if Q_SUBTILES % 2:
        carry = lax.cond(n_past % 2 == 1, lambda c: consume(n_past - 1, 0, c), lambda c: c, carry)

    for sub in range(Q_SUBTILES):
        outs = []
        for head in range(HEADS_PER_GROUP):
            acc = carry[2 * (sub * HEADS_PER_GROUP + head) + 1]
            outs.append(acc[:HEAD_DIM] / acc[HEAD_DIM:HEAD_DIM + 1])
        o_ref[sub * tq:(sub + 1) * tq, :] = jnp.concatenate(outs, axis=0).T.astype(o_ref.dtype)


def _prompt_attn(qt, k, vt, kbias, side, kstat=None, *, moba, n_fox):
    bsz, w, l = qt.shape
    tq = Q_SUBTILES * MOBA_BLOCK
    assert l % tq == 0
    groups = w // LANES
    n_chain = Q_SUBTILES * HEADS_PER_GROUP
    if moba:
        kbias_spec = pl.BlockSpec((l, LANES), lambda b, g, i: (0, 0))
        side_spec = pl.BlockSpec((None, LANES, LANES), lambda b, g, i: (b, 0, g))
        extra_specs, extra_args = [], []
    else:
        kbias_spec = pl.BlockSpec((None, l, LANES), lambda b, g, i: (b, 0, 0))
        side = side.reshape(bsz, groups, HEADS_PER_GROUP, l)
        side_spec = pl.BlockSpec((None, None, HEADS_PER_GROUP, l), lambda b, g, i: (b, g, 0, 0))
        n_blocks = kstat.shape[1]
        assert n_blocks <= LANES
        kstat = jnp.transpose(kstat[:, :, :, :groups * HEADS_PER_GROUP], (0, 3, 2, 1))
        kstat = jnp.pad(kstat, ((0, 0), (0, 0), (0, 0), (0, LANES - n_blocks)))
        kstat = kstat.reshape(bsz, groups, 2 * HEADS_PER_GROUP, LANES)
        extra_specs = [pl.BlockSpec((None, None, 2 * HEADS_PER_GROUP, LANES), lambda b, g, i: (b, g, 0, 0))]
        extra_args = [kstat]
    return pl.pallas_call(
        functools.partial(_prompt_attn_kernel, moba=moba, n_fox=n_fox),
        grid=(bsz, groups, l // tq),
        in_specs=[pl.BlockSpec((None, LANES, tq), lambda b, g, i: (b, g, i)),
                  pl.BlockSpec((None, l, LANES), lambda b, g, i: (b, 0, g)),
                  pl.BlockSpec((None, LANES, l), lambda b, g, i: (b, g, 0)),
                  kbias_spec, side_spec, *extra_specs],
        out_specs=pl.BlockSpec((None, tq, LANES), lambda b, g, i: (b, i, g)),
        out_shape=jax.ShapeDtypeStruct((bsz, l, w), BF16),
        scratch_shapes=([pltpu.VMEM((MOBA_BLOCK, MOBA_BLOCK), F32)] * (2 * n_chain)
                        + [pltpu.VMEM((8, MOBA_BLOCK), F32)] * (2 * n_chain)),
        compiler_params=_params("parallel", "parallel", "arbitrary"),
        name="moba_prompt" if moba else "fox_prompt",
    )(qt, k, vt, kbias, side, *extra_args)


PAGES_PER_STEP = 32


def _lane_bcast(col, n):
    return jnp.broadcast_to(col, (col.shape[0], n))


def _page_scores(qb_scr, k_refs):
    rows = []
    for head in range(qb_scr.shape[0]):
        qh = qb_scr[head]
        rows.append(jnp.concatenate([jnp.sum(r[head] * qh, axis=0, keepdims=True) for r in k_refs], axis=1))
    return jnp.concatenate(rows, axis=0)


def _accumulate_values(acc_scr, v_refs, p, alpha):
    page = acc_scr.shape[2]
    for head in range(acc_scr.shape[0]):
        a = acc_scr[head]
        if alpha is not None:
            a = a * alpha[head:head + 1, :]
        for i, r in enumerate(v_refs):
            a = a + r[head] * p[head:head + 1, i * page:(i + 1) * page]
        acc_scr[head] = a


def _finish_rows(acc_scr, l):
    h, hd, page = acc_scr.shape
    parts = [acc_scr[head] / l[head:head + 1, :] for head in range(h)]
    return jnp.sum(jnp.concatenate(parts, axis=0).T, axis=0, keepdims=True)


def _new_token_values(vnt_ref, acc_scr, weight):
    h, hd, page = acc_scr.shape
    lane0 = lax.broadcasted_iota(jnp.int32, (hd, page), 1) == 0
    for head in range(h):
        col = vnt_ref[:, head:head + 1]
        if weight is not None:
            col = col * weight[head:head + 1, :]
        acc_scr[head] = jnp.where(lane0, _lane_bcast(col, page), 0.0)


def _fox_sample_kernel(pt_ref, q_ref, qt_ref, kn_ref, vnt_ref, lfn_ref, *rest):
    pps = PAGES_PER_STEP
    k_refs, v_refs, lf_refs = rest[:pps], rest[pps:2 * pps], rest[2 * pps:3 * pps]
    o_ref, qb_scr, m_scr, l_scr, acc_scr, carry_scr = rest[3 * pps:]
    c = pl.program_id(1)
    page = acc_scr.shape[2]

    @pl.when(c == 0)
    def _():
        for head in range(qb_scr.shape[0]):
            qb_scr[head] = _lane_bcast(qt_ref[:, head:head + 1] * SCALE, page)
        m_scr[...] = jnp.sum(q_ref[...] * SCALE * kn_ref[...], axis=1, keepdims=True)
        l_scr[...] = jnp.ones_like(l_scr)
        _new_token_values(vnt_ref, acc_scr, None)
        carry_scr[...] = lfn_ref[...]

    lf = jnp.concatenate([r[...] for r in lf_refs], axis=1)
    nk = lf.shape[1]
    lane = lax.broadcasted_iota(jnp.int32, lf.shape, 1)
    sfx = jnp.where(lane < nk - 1, pltpu.roll(lf, nk - 1, 1), 0.0)
    d = 1
    while d < nk:
        sfx = sfx + jnp.where(lane < nk - d, pltpu.roll(sfx, nk - d, 1), 0.0)
        d *= 2
    carry = carry_scr[...]
    s = _page_scores(qb_scr, k_refs) + (carry + sfx)
    carry_scr[...] = carry + jnp.sum(lf, axis=1, keepdims=True)
    m = m_scr[...]
    m_new = jnp.maximum(m, jnp.max(s, axis=1, keepdims=True))
    alpha = jnp.exp(m - m_new)
    p = jnp.exp(s - m_new)
    m_scr[...] = m_new
    l_scr[...] = alpha * l_scr[...] + jnp.sum(p, axis=1, keepdims=True)
    _accumulate_values(acc_scr, v_refs, p, alpha)

    @pl.when(c == pl.num_programs(1) - 1)
    def _():
        o_ref[...] = _finish_rows(acc_scr, l_scr[...])


def _sample_row_specs(h, hd):
    nat = pl.BlockSpec((None, h, hd), lambda s, c, pt: (s, 0, 0))
    tr = pl.BlockSpec((None, hd, h), lambda s, c, pt: (s, 0, 0))
    return nat, tr


def _fox_sample(page_table, q, k_new, v_new, lf_new, cache_kt, cache_vt, cache_lft):
    s_n, h, hd = q.shape
    n_pages = page_table.shape[1]
    pps = PAGES_PER_STEP
    assert n_pages % pps == 0
    nc = n_pages // pps

    def page(i, nd):
        return lambda s, c, pt: (pt[s * n_pages + (n_pages - (c + 1) * pps + i)],) + (0,) * nd

    nat, tr = _sample_row_specs(h, hd)
    in_specs = [nat, tr, nat, tr, pl.BlockSpec((None, h, 1), lambda s, c, pt: (s, 0, 0))]
    in_specs += [pl.BlockSpec((None, h, hd, PAGE_SIZE), page(i, 3)) for i in range(pps)]
    in_specs += [pl.BlockSpec((None, h, hd, PAGE_SIZE), page(i, 3)) for i in range(pps)]
    in_specs += [pl.BlockSpec((None, h, PAGE_SIZE), page(i, 2)) for i in range(pps)]
    out = pl.pallas_call(
        _fox_sample_kernel,
        grid_spec=pltpu.PrefetchScalarGridSpec(
            num_scalar_prefetch=1, grid=(s_n, nc), in_specs=in_specs,
            out_specs=pl.BlockSpec((None, 1, h * hd), lambda s, c, pt: (s, 0, 0)),
            scratch_shapes=[pltpu.VMEM((h, hd, PAGE_SIZE), F32), pltpu.VMEM((h, 1), F32), pltpu.VMEM((h, 1), F32),
                            pltpu.VMEM((h, hd, PAGE_SIZE), F32), pltpu.VMEM((h, 1), F32)]),
        out_shape=jax.ShapeDtypeStruct((s_n, 1, h * hd), F32),
        compiler_params=_params("parallel", "arbitrary"),
        name="fox_sample",
    )(page_table.reshape(-1), q, jnp.swapaxes(q, 1, 2), k_new, jnp.swapaxes(v_new, 1, 2),
      lf_new.reshape(s_n, h, 1), *([cache_kt] * pps), *([cache_vt] * pps), *([cache_lft] * pps))
    return out.reshape(s_n, h * hd)


def _moba_sample_kernel(pt_ref, q_ref, qt_ref, kn_ref, vnt_ref, *rest):
    pps = PAGES_PER_STEP
    k_refs, v_refs = rest[:pps], rest[pps:2 * pps]
    o_ref, qb_scr, s_scr, p_scr, l_scr, acc_scr = rest[2 * pps:]
    c = pl.program_id(1)
    nc = s_scr.shape[0]
    nk = s_scr.shape[2]
    page = acc_scr.shape[2]

    @pl.when(c == 0)
    def _():
        for head in range(qb_scr.shape[0]):
            qb_scr[head] = _lane_bcast(qt_ref[:, head:head + 1], page)

    @pl.when(c < nc)
    def _():
        s_scr[c] = _page_scores(qb_scr, k_refs)

    @pl.when(c == nc - 1)
    def _():
        bpc = nk // MOBA_BLOCK
        gates = []
        for cc in range(nc):
            sc = s_scr[cc]
            for b in range(bpc):
                seg = sc[:, b * MOBA_BLOCK:(b + 1) * MOBA_BLOCK]
                gates.append(jnp.broadcast_to(jnp.sum(seg, axis=1, keepdims=True) * (1.0 / MOBA_BLOCK),
                                              seg.shape))
        gate = jnp.concatenate(gates, axis=1)
        blk = (lax.broadcasted_iota(jnp.int32, gate.shape, 1) // MOBA_BLOCK).astype(F32)
        n_blk = float(gate.shape[1] // MOBA_BLOCK)
        sel = jnp.zeros(gate.shape, jnp.bool_)
        for _ in range(MOBA_TOPK):
            best = jnp.max(gate, axis=1, keepdims=True)
            pick = jnp.min(jnp.where(gate == best, blk, n_blk), axis=1, keepdims=True)
            chosen = blk == pick
            sel = sel | chosen
            gate = jnp.where(chosen, -jnp.inf, gate)
        s_self = jnp.sum(q_ref[...] * kn_ref[...], axis=1, keepdims=True) * SCALE
        s_all = jnp.concatenate([s_scr[cc] for cc in range(nc)], axis=1) * SCALE
        s_all = jnp.where(sel, s_all, NEG_INF)
        m = jnp.maximum(jnp.max(s_all, axis=1, keepdims=True), s_self)
        p = jnp.exp(s_all - m)
        p_self = jnp.exp(s_self - m)
        l_scr[...] = jnp.sum(p, axis=1, keepdims=True) + p_self
        _new_token_values(vnt_ref, acc_scr, p_self)
        for cc in range(nc):
            p_scr[cc] = p[:, cc * nk:(cc + 1) * nk]

    @pl.when(c >= nc)
    def _():
        _accumulate_values(acc_scr, v_refs, p_scr[c - nc], None)

    @pl.when(c == 2 * nc - 1)
    def _():
        o_ref[...] = _finish_rows(acc_scr, l_scr[...])


def _moba_sample(page_table, q, k_new, v_new, cache_kt, cache_vt):
    s_n, h, hd = q.shape
    n_pages = page_table.shape[1]
    pps = PAGES_PER_STEP
    assert n_pages % pps == 0
    nc = n_pages // pps
    nk = pps * PAGE_SIZE
    assert nk % MOBA_BLOCK == 0 and n_pages * PAGE_SIZE // MOBA_BLOCK >= MOBA_TOPK

    def kpage(i):
        return lambda s, c, pt: (pt[s * n_pages + jnp.minimum(c, nc - 1) * pps + i],) + (0,) * 3

    def vpage(i):
        return lambda s, c, pt: (pt[s * n_pages + jnp.maximum(c - nc, 0) * pps + i],) + (0,) * 3

    nat, tr = _sample_row_specs(h, hd)
    in_specs = [nat, tr, nat, tr]
    in_specs += [pl.BlockSpec((None, h, hd, PAGE_SIZE), kpage(i)) for i in range(pps)]
    in_specs += [pl.BlockSpec((None, h, hd, PAGE_SIZE), vpage(i)) for i in range(pps)]
    out = pl.pallas_call(
        _moba_sample_kernel,
        grid_spec=pltpu.PrefetchScalarGridSpec(
            num_scalar_prefetch=1, grid=(s_n, 2 * nc), in_specs=in_specs,
            out_specs=pl.BlockSpec((None, 1, h * hd), lambda s, c, pt: (s, 0, 0)),
            scratch_shapes=[pltpu.VMEM((h, hd, PAGE_SIZE), F32), pltpu.VMEM((nc, h, nk), F32),
                            pltpu.VMEM((nc, h, nk), F32), pltpu.VMEM((h, 1), F32),
                            pltpu.VMEM((h, hd, PAGE_SIZE), F32)]),
        out_shape=jax.ShapeDtypeStruct((s_n, 1, h * hd), F32),
        compiler_params=_params("parallel", "arbitrary"),
        name="moba_sample",
    )(page_table.reshape(-1), q, jnp.swapaxes(q, 1, 2), k_new, jnp.swapaxes(v_new, 1, 2),
      *([cache_kt] * pps), *([cache_vt] * pps))
    return out.reshape(s_n, h * hd)


def _merge_kernel(h_ref, oa_ref, ob_ref, gl_ref, ga_ref, woa_ref, wob_ref, wout_ref, o_ref):
    d = h_ref.shape[1]
    ya = _dot(oa_ref[...].astype(BF16), woa_ref[...])
    yb = _dot(ob_ref[...].astype(BF16), wob_ref[...])
    gl = gl_ref[...]
    merged = jax.nn.sigmoid(gl[:, :d]) * ya + jax.nn.sigmoid(gl[:, d:]) * yb
    o_ref[...] = h_ref[...] + ga_ref[...] * _dot(merged.astype(BF16), wout_ref[...])


def _merge(h, oa, ob, gl, mod, woa, wob, wout, tokens_per_seq):
    t, d = h.shape
    tm = min(t, 512)
    tps = max(tokens_per_seq // tm, 1)
    tok = lambda i: (i, 0)
    const = lambda i: (0, 0)
    return pl.pallas_call(
        _merge_kernel,
        grid=(t // tm,),
        in_specs=[pl.BlockSpec((tm, d), tok), pl.BlockSpec((tm, oa.shape[1]), tok),
                  pl.BlockSpec((tm, ob.shape[1]), tok), pl.BlockSpec((tm, gl.shape[1]), tok),
                  _mod_spec(mod, 5, tm, tps),
                  pl.BlockSpec(woa.shape, const), pl.BlockSpec(wob.shape, const), pl.BlockSpec(wout.shape, const)],
        out_specs=pl.BlockSpec((tm, d), tok),
        out_shape=jax.ShapeDtypeStruct((t, d), F32),
        compiler_params=_params("parallel"),
        name="merge",
    )(h, oa, ob, gl, mod, woa, wob, wout)


def _rope_tables(pos):
    half = HEAD_DIM // 2
    inv = ROPE_THETA ** (-jnp.arange(half, dtype=F32) / half)
    ang = pos.astype(F32)[:, None] * inv[None, :]
    cos, sin = jnp.cos(ang), jnp.sin(ang)
    reps = LANES // HEAD_DIM
    return (jnp.tile(jnp.concatenate([cos, cos], axis=1), (1, reps)),
            jnp.tile(jnp.concatenate([-sin, sin], axis=1), (1, reps)))


def kernel(x_prompt, x_sample, cache_moba_k, cache_moba_v, cache_fox_k, cache_fox_v, cache_fox_logf, page_table,
           c_prompt, c_sample, w_ada, b_ada, g_ff1, w_ff1_in, w_ff1_out, g_mix, w_mix, b_forget, w_o_moba,
           w_o_fox, w_out, g_ff2, w_ff2_in, w_ff2_out, g_final):
    bsz, l, d = x_prompt.shape
    s_n, dec_seq, _ = x_sample.shape
    assert dec_seq == 1
    depth = w_ada.shape[0]
    h_moba, h_fox = cache_moba_k.shape[3], cache_fox_k.shape[3]
    w_moba, w_fox = h_moba * HEAD_DIM, h_fox * HEAD_DIM
    past_len = page_table.shape[1] * PAGE_SIZE
    n_qkv = 3 * w_moba + 3 * w_fox
    n_blocks = l // MOBA_BLOCK
    assert l % MOBA_BLOCK == 0 and MOBA_LANE0 + n_blocks <= LANES

    cos_p, sin_p = _rope_tables(jnp.arange(l, dtype=jnp.int32))
    cos_s, sin_s = _rope_tables(jnp.full((1,), past_len, jnp.int32))
    pad_rows = (-(s_n + bsz)) % 8
    c_all = jnp.concatenate([c_sample, c_prompt, jnp.zeros((pad_rows, d), F32)], axis=0)
    block_lanes = (jnp.arange(LANES, dtype=jnp.int32)[None, :] - MOBA_LANE0
                   == jnp.arange(l, dtype=jnp.int32)[:, None] // MOBA_BLOCK).astype(BF16)

    hp = x_prompt.reshape(bsz * l, d)
    hs = x_sample.reshape(s_n, d)
    st_p = [[] for _ in range(5)]
    st_s = [[] for _ in range(5)]
    for layer in range(depth):
        mod = _modulation(c_all, w_ada[layer], b_ada[layer])
        mod_s = mod[:, :s_n]
        mod_p = mod[:, s_n:s_n + bsz].reshape(N_MOD, bsz, 1, d)
        w1i, w1o = w_ff1_in[layer].astype(BF16), w_ff1_out[layer].astype(BF16)
        w2i, w2o = w_ff2_in[layer].astype(BF16), w_ff2_out[layer].astype(BF16)
        wm = w_mix[layer]
        wqkv = wm[:, :n_qkv].astype(BF16)
        wf = jnp.pad(wm[:, n_qkv:n_qkv + h_fox], ((0, 0), (0, LANES - h_fox))).astype(BF16)
        wg = wm[:, n_qkv + h_fox:].astype(BF16)
        bfor = jnp.pad(b_forget[layer], (0, LANES - h_fox)).reshape(1, LANES)
        woa, wob, wout = w_o_moba[layer].astype(BF16), w_o_fox[layer].astype(BF16), w_out[layer].astype(BF16)
        last = layer == depth - 1
        mix_kw = dict(w_moba=w_moba, w_fox=w_fox, n_fox=h_fox)

        hp = _ffn(hp, mod_p, 0, g_ff1[layer], w1i, w1o, l)
        (kat, vat, kbt, vbt, lft, gl, ka16, kb16, qat16, vat16, qbt16, vbt16, kbias, cumt, km, kstat) = _mix(
            hp, mod_p, g_mix[layer], cos_p, sin_p, wqkv, wf, bfor, wg, seq_shape=(bsz, l), **mix_kw)
        km = jnp.pad(km.reshape(bsz, n_blocks, w_moba), ((0, 0), (MOBA_LANE0, LANES - MOBA_LANE0 - n_blocks), (0, 0)))
        oa = _prompt_attn(qat16, ka16.reshape(bsz, l, w_moba), vat16, block_lanes, km, moba=True, n_fox=h_fox)
        ob = _prompt_attn(qbt16, kb16.reshape(bsz, l, w_fox), vbt16, kbias.reshape(bsz, l, LANES), cumt, kstat,
                          moba=False, n_fox=h_fox)
        hp = _merge(hp, oa.reshape(bsz * l, w_moba), ob.reshape(bsz * l, w_fox), gl, mod_p, woa, wob, wout, l)
        hp = _ffn(hp, mod_p, 6, g_ff2[layer], w2i, w2o, l, g_final=g_final if last else None)
        heads = lambda a, h: jnp.transpose(a.reshape(bsz, h, HEAD_DIM, l), (0, 3, 1, 2))
        for i, a in enumerate((heads(kat, h_moba), heads(vat, h_moba), heads(kbt, h_fox), heads(vbt, h_fox),
                               jnp.swapaxes(lft, 1, 2))):
            st_p[i].append(a)

        hs = _ffn(hs, mod_s, 0, g_ff1[layer], w1i, w1o, 1)
        ka, va, kb, vb, lf, gl, qa, qb = _mix(hs, mod_s, g_mix[layer], cos_s, sin_s, wqkv, wf, bfor, wg,
                                              seq_shape=None, **mix_kw)
        ka, va, qa = (a.reshape(s_n, h_moba, HEAD_DIM) for a in (ka, va, qa))
        kb, vb, qb = (a.reshape(s_n, h_fox, HEAD_DIM) for a in (kb, vb, qb))
        pages_t = lambda a: jnp.transpose(a[layer], (0, 2, 3, 1))
        oa = _moba_sample(page_table, qa, ka, va, pages_t(cache_moba_k), pages_t(cache_moba_v))
        ob = _fox_sample(page_table, qb, kb, vb, lf, pages_t(cache_fox_k), pages_t(cache_fox_v),
                         jnp.swapaxes(cache_fox_logf[layer], 1, 2))
        hs = _merge(hs, oa, ob, gl, mod_s, woa, wob, wout, 1)
        hs = _ffn(hs, mod_s, 6, g_ff2[layer], w2i, w2o, 1, g_final=g_final if last else None)
        for i, a in enumerate((ka.reshape(s_n, 1, h_moba, HEAD_DIM), va.reshape(s_n, 1, h_moba, HEAD_DIM),
                               kb.reshape(s_n, 1, h_fox, HEAD_DIM), vb.reshape(s_n, 1, h_fox, HEAD_DIM),
                               lf.reshape(s_n, 1, h_fox))):
            st_s[i].append(a)

    return (hp.reshape(bsz, l, d), hs.reshape(s_n, 1, d),
            *(jnp.stack(a) for a in st_p), *(jnp.stack(a) for a in st_s))
```

```python
import functools
import math

import jax
import jax.numpy as jnp
from jax import lax
from jax.experimental import pallas as pl
from jax.experimental.pallas import tpu as pltpu

HEAD_DIM = 64
MOBA_BLOCK = 256
MOBA_TOPK = 3
PAGE_SIZE = 128
ROPE_THETA = 10000.0
NORM_EPS = 1e-6
NEG_INF = -1e30
N_MOD = 9
LANES = 128
HEADS_PER_GROUP = LANES // HEAD_DIM
VMEM_LIMIT = 56 * 1024 * 1024

F32 = jnp.float32
BF16 = jnp.bfloat16
SCALE = 1.0 / math.sqrt(HEAD_DIM)

N_PIECES = 3
MOBA_LANE0 = 32
Q_SUBTILES = 2
EXP_ZERO_GAP = 110.0
BOUND_SLACK_MUL = 1.02
BOUND_SLACK_ADD = 0.01


def _params(*sem):
    return pltpu.CompilerParams(dimension_semantics=sem, vmem_limit_bytes=VMEM_LIMIT)


def _dot(a, b):
    return jnp.dot(a, b, preferred_element_type=F32)


def _dot_nt(a, b):
    return lax.dot_general(a, b, (((1,), (1,)), ((), ())), preferred_element_type=F32)


def _rms(x, g):
    return x * lax.rsqrt(jnp.mean(x * x, axis=-1, keepdims=True) + NORM_EPS) * g


def _silu(x):
    return x * jax.nn.sigmoid(x)


def _split_bf16(x):
    hi = x.astype(BF16).astype(F32)
    r = x - hi
    mid = r.astype(BF16).astype(F32)
    lo = (r - mid).astype(BF16).astype(F32)
    return hi, mid, lo


def _mod_kernel(c_ref, w_ref, b_ref, o_ref):
    a = _silu(c_ref[...]).astype(BF16)
    o_ref[...] = _dot(a, w_ref[...].astype(BF16)) + b_ref[...]


def _modulation(c, w_ada, b_ada):
    r, d = c.shape
    return pl.pallas_call(
        _mod_kernel,
        grid=(N_MOD,),
        in_specs=[pl.BlockSpec((r, d), lambda j: (0, 0)),
                  pl.BlockSpec((d, d), lambda j: (0, j)),
                  pl.BlockSpec((1, d), lambda j: (0, j))],
        out_specs=pl.BlockSpec((None, r, d), lambda j: (j, 0, 0)),
        out_shape=jax.ShapeDtypeStruct((N_MOD, r, d), F32),
        compiler_params=_params("arbitrary"),
        name="mod",
    )(c, w_ada, b_ada.reshape(1, -1))


def _mod_spec(mod, k, tm, tiles_per_seq):
    d = mod.shape[-1]
    if mod.ndim == 4:
        return pl.BlockSpec((None, None, 1, d), lambda i, *_: (k, i // tiles_per_seq, 0, 0))
    return pl.BlockSpec((None, tm, d), lambda i, *_: (k, i, 0))


def _ffn_kernel(x_ref, sh_ref, sc_ref, ga_ref, g_ref, wa_ref, wb_ref, wo_ref, *rest, final_norm):
    if final_norm:
        gf_ref, o_ref, n_scr, acc_scr = rest
    else:
        o_ref, n_scr, acc_scr = rest
    c = pl.program_id(1)

    @pl.when(c == 0)
    def _():
        n = _rms(x_ref[...], g_ref[...]) * (1 + sc_ref[...]) + sh_ref[...]
        n_scr[...] = n.astype(BF16)
        acc_scr[...] = jnp.zeros_like(acc_scr)

    n = n_scr[...]
    a = _dot(n, wa_ref[...])
    b = _dot(n, wb_ref[...])
    acc_scr[...] += _dot((_silu(a) * b).astype(BF16), wo_ref[...])

    @pl.when(c == pl.num_programs(1) - 1)
    def _():
        h = x_ref[...] + 0.5 * ga_ref[...] * acc_scr[...]
        if final_norm:
            h = _rms(h, gf_ref[...])
        o_ref[...] = h


def _ffn_tiles(t, d_ff):
    tm = min(t, 1024)
    tf = d_ff
    for parts in (4, 2):
        if d_ff % (parts * LANES) == 0:
            tf = d_ff // parts
            break
    return tm, tf


def _ffn(x, mod, k0, g, w_in, w_out, tiles_per_seq_tokens, g_final=None):
    t, d = x.shape
    d_ff = w_out.shape[0]
    tm, tf = _ffn_tiles(t, d_ff)
    nc = d_ff // tf
    tps = max(tiles_per_seq_tokens // tm, 1)
    in_specs = [pl.BlockSpec((tm, d), lambda i, c: (i, 0)),
                _mod_spec(mod, k0, tm, tps), _mod_spec(mod, k0 + 1, tm, tps), _mod_spec(mod, k0 + 2, tm, tps),
                pl.BlockSpec((1, d), lambda i, c: (0, 0)),
                pl.BlockSpec((d, tf), lambda i, c: (0, c)),
                pl.BlockSpec((d, tf), lambda i, c: (0, c + nc)),
                pl.BlockSpec((tf, d), lambda i, c: (c, 0))]
    args = [x, mod, mod, mod, g.reshape(1, d), w_in, w_in, w_out]
    if g_final is not None:
        in_specs.append(pl.BlockSpec((1, d), lambda i, c: (0, 0)))
        args.append(g_final.reshape(1, d))
    return pl.pallas_call(
        functools.partial(_ffn_kernel, final_norm=g_final is not None),
        grid=(t // tm, nc),
        in_specs=in_specs,
        out_specs=pl.BlockSpec((tm, d), lambda i, c: (i, 0)),
        out_shape=jax.ShapeDtypeStruct((t, d), F32),
        scratch_shapes=[pltpu.VMEM((tm, d), BF16), pltpu.VMEM((tm, d), F32)],
        compiler_params=_params("parallel", "arbitrary"),
        name="ffn_final" if g_final is not None else "ffn",
    )(*args)


def _rope(x, cos, sin_signed):
    w = x.shape[1]
    reps = w // LANES
    cos = jnp.concatenate([cos] * reps, axis=1)
    sin_signed = jnp.concatenate([sin_signed] * reps, axis=1)
    lane = lax.broadcasted_iota(jnp.int32, x.shape, 1)
    first_half = (lane % HEAD_DIM) < (HEAD_DIM // 2)
    swapped = jnp.where(first_half, pltpu.roll(x, w - HEAD_DIM // 2, 1), pltpu.roll(x, HEAD_DIM // 2, 1))
    return x * cos + swapped * sin_signed


def _mix_kernel(h_ref, sh_ref, sc_ref, g_ref, cos_ref, sin_ref, wqkv_ref, wf_ref, bf_ref, wg_ref, *rest,
                w_moba, w_fox, n_fox, seq):
    if seq:
        (kat_ref, vat_ref, kbt_ref, vbt_ref, lft_ref, gl_ref, ka16_ref, kb16_ref, qat16_ref, vat16_ref, qbt16_ref,
         vbt16_ref, kbias_ref, cumt_ref, km_ref, kstat_ref, carry_scr) = rest
    else:
        ka_ref, va_ref, kb_ref, vb_ref, lf_ref, gl_ref, qa_ref, qb_ref = rest
    n = (_rms(h_ref[...], g_ref[...]) * (1 + sc_ref[...]) + sh_ref[...]).astype(BF16)
    z = _dot(n, wqkv_ref[...])
    o = 0
    qa = _rope(z[:, o:o + w_moba], cos_ref[...], sin_ref[...]); o += w_moba
    ka = _rope(z[:, o:o + w_moba], cos_ref[...], sin_ref[...]); o += w_moba
    va = z[:, o:o + w_moba]; o += w_moba
    qb = z[:, o:o + w_fox]; o += w_fox
    kb = z[:, o:o + w_fox]; o += w_fox
    vb = z[:, o:o + w_fox]
    lf = jax.nn.log_sigmoid(_dot(n, wf_ref[...]) + bf_ref[...])
    gl_ref[...] = _dot(n, wg_ref[...])
    if not seq:
        ka_ref[...] = ka
        va_ref[...] = va
        kb_ref[...] = kb
        vb_ref[...] = vb
        lf_ref[...] = lf[:, :n_fox]
        qa_ref[...] = qa
        qb_ref[...] = qb
        return
    kat, vat, kbt, vbt = ka.T, va.T, kb.T, vb.T
    kat_ref[...] = kat
    vat_ref[...] = vat
    kbt_ref[...] = kbt
    vbt_ref[...] = vbt
    lft_ref[...] = lf.T[:n_fox, :]
    ka16_ref[...] = ka.astype(BF16)
    kb16_ref[...] = kb.astype(BF16)
    qat16_ref[...] = (qa * SCALE).T.astype(BF16)
    vat16_ref[...] = vat.astype(BF16)
    qbt16_ref[...] = (qb * SCALE).T.astype(BF16)
    vbt16_ref[...] = vbt.astype(BF16)
    tm = lf.shape[0]

    @pl.when(pl.program_id(1) == 0)
    def _():
        carry_scr[...] = jnp.zeros_like(carry_scr)

    row = lax.broadcasted_iota(jnp.int32, (tm, tm), 0)
    col = lax.broadcasted_iota(jnp.int32, (tm, tm), 1)
    tri = (row >= col).astype(F32)
    cum = jnp.dot(tri, lf, precision=lax.Precision.HIGHEST, preferred_element_type=F32) + carry_scr[...]
    carry_scr[...] = cum[tm - 1:tm, :]
    cumt_ref[...] = cum.T[:n_fox, :]
    hi, mid, lo = _split_bf16(cum)
    lane = lax.broadcasted_iota(jnp.int32, cum.shape, 1)
    kbias = jnp.where(lane < n_fox, -hi,
                      jnp.where(lane < 2 * n_fox, -pltpu.roll(mid, n_fox, 1),
                                jnp.where(lane < 3 * n_fox, -pltpu.roll(lo, 2 * n_fox, 1),
                                          jnp.where(lane < 3 * n_fox + N_PIECES, 1.0, 0.0))))
    kbias_ref[...] = kbias.astype(BF16)
    km_ref[...] = jnp.sum(ka, axis=0, keepdims=True) * (1.0 / MOBA_BLOCK)
    head_of = lax.broadcasted_iota(jnp.int32, (w_fox, LANES), 0) // HEAD_DIM
    seg = (head_of == lax.broadcasted_iota(jnp.int32, (w_fox, LANES), 1)).astype(BF16)
    norm2 = jnp.max(_dot((kb * kb).astype(BF16), seg), axis=0, keepdims=True)
    kstat_ref[...] = jnp.concatenate([norm2, jnp.min(cum, axis=0, keepdims=True)], axis=0)


def _mix(h, mod, g, cos, sin_signed, wqkv, wf, bf, wg, *, w_moba, w_fox, n_fox, seq_shape):
    t, d = h.shape
    seq = seq_shape is not None
    tm = min(t, MOBA_BLOCK)
    if seq:
        bsz, l = seq_shape
        nt = l // tm
        grid = (bsz, nt)
        tok = lambda b, i: (b * nt + i, 0)
        rope_spec = pl.BlockSpec((tm, LANES), lambda b, i: (i, 0))
        mod_specs = [pl.BlockSpec((None, None, 1, d), functools.partial(lambda b, i, k: (k, b, 0, 0), k=k))
                     for k in (3, 4)]
        sem = ("parallel", "arbitrary")
    else:
        grid = (t // tm,)
        tok = lambda i: (i, 0)
        rope_spec = pl.BlockSpec((1, LANES), lambda i: (0, 0))
        mod_specs = [pl.BlockSpec((None, tm, d), functools.partial(lambda i, k: (k, i, 0), k=k)) for k in (3, 4)]
        sem = ("parallel",)
    const = lambda *_: (0, 0)
    in_specs = [pl.BlockSpec((tm, d), tok), *mod_specs,
                pl.BlockSpec((1, d), const), rope_spec, rope_spec,
                pl.BlockSpec(wqkv.shape, const), pl.BlockSpec(wf.shape, const), pl.BlockSpec(bf.shape, const),
                pl.BlockSpec(wg.shape, const)]
    scratch = []
    if seq:
        assert tm == MOBA_BLOCK and 3 * n_fox + N_PIECES <= LANES
        tr = lambda b, i: (b, 0, i)
        out_specs = [pl.BlockSpec((None, w_moba, tm), tr), pl.BlockSpec((None, w_moba, tm), tr),
                     pl.BlockSpec((None, w_fox, tm), tr), pl.BlockSpec((None, w_fox, tm), tr),
                     pl.BlockSpec((None, n_fox, tm), tr), pl.BlockSpec((tm, wg.shape[1]), tok),
                     pl.BlockSpec((tm, w_moba), tok), pl.BlockSpec((tm, w_fox), tok),
                     pl.BlockSpec((None, w_moba, tm), tr), pl.BlockSpec((None, w_moba, tm), tr),
                     pl.BlockSpec((None, w_fox, tm), tr), pl.BlockSpec((None, w_fox, tm), tr),
                     pl.BlockSpec((tm, LANES), tok),
                     pl.BlockSpec((None, n_fox, tm), tr),
                     pl.BlockSpec((None, None, 1, w_moba), lambda b, i: (b, i, 0, 0)),
                     pl.BlockSpec((None, None, 2, LANES), lambda b, i: (b, i, 0, 0))]
        out_shape = [jax.ShapeDtypeStruct((bsz, w_moba, l), F32), jax.ShapeDtypeStruct((bsz, w_moba, l), F32),
                     jax.ShapeDtypeStruct((bsz, w_fox, l), F32), jax.ShapeDtypeStruct((bsz, w_fox, l), F32),
                     jax.ShapeDtypeStruct((bsz, n_fox, l), F32), jax.ShapeDtypeStruct((t, wg.shape[1]), F32),
                     jax.ShapeDtypeStruct((t, w_moba), BF16), jax.ShapeDtypeStruct((t, w_fox), BF16),
                     jax.ShapeDtypeStruct((bsz, w_moba, l), BF16), jax.ShapeDtypeStruct((bsz, w_moba, l), BF16),
                     jax.ShapeDtypeStruct((bsz, w_fox, l), BF16), jax.ShapeDtypeStruct((bsz, w_fox, l), BF16),
                     jax.ShapeDtypeStruct((t, LANES), BF16),
                     jax.ShapeDtypeStruct((bsz, n_fox, l), F32),
                     jax.ShapeDtypeStruct((bsz, nt, 1, w_moba), F32),
                     jax.ShapeDtypeStruct((bsz, nt, 2, LANES), F32)]
        scratch = [pltpu.VMEM((1, LANES), F32)]
    else:
        out_specs = [pl.BlockSpec((tm, w_moba), tok), pl.BlockSpec((tm, w_moba), tok),
                     pl.BlockSpec((tm, w_fox), tok), pl.BlockSpec((tm, w_fox), tok),
                     pl.BlockSpec((tm, n_fox), tok), pl.BlockSpec((tm, wg.shape[1]), tok),
                     pl.BlockSpec((tm, w_moba), tok), pl.BlockSpec((tm, w_fox), tok)]
        out_shape = [jax.ShapeDtypeStruct((t, w_moba), F32), jax.ShapeDtypeStruct((t, w_moba), F32),
                     jax.ShapeDtypeStruct((t, w_fox), F32), jax.ShapeDtypeStruct((t, w_fox), F32),
                     jax.ShapeDtypeStruct((t, n_fox), F32), jax.ShapeDtypeStruct((t, wg.shape[1]), F32),
                     jax.ShapeDtypeStruct((t, w_moba), F32), jax.ShapeDtypeStruct((t, w_fox), F32)]
    return pl.pallas_call(
        functools.partial(_mix_kernel, w_moba=w_moba, w_fox=w_fox, n_fox=n_fox, seq=seq),
        grid=grid, in_specs=in_specs, out_specs=out_specs, out_shape=out_shape, scratch_shapes=scratch,
        compiler_params=_params(*sem),
        name="mix_seq" if seq else "mix_tok",
    )(h, mod, mod, g.reshape(1, d), cos, sin_signed, wqkv, wf, bf, wg)


def _prompt_attn_kernel(qt_ref, k_ref, vt_ref, kbias_ref, side_ref, *rest, moba, n_fox):
    if moba:
        o_ref, *scratch = rest
    else:
        kstat_ref, o_ref, *scratch = rest
    g = pl.program_id(1)
    i = pl.program_id(2)
    tq = MOBA_BLOCK
    chains = [(sub, head) for sub in range(Q_SUBTILES) for head in range(HEADS_PER_GROUP)]
    n_chain = len(chains)
    s_bufs = [scratch[b * n_chain:(b + 1) * n_chain] for b in range(2)]
    mx_bufs = [scratch[(2 + b) * n_chain:(3 + b) * n_chain] for b in range(2)]
    wrow = lax.broadcasted_iota(jnp.int32, (LANES, tq), 0)
    ones_rows = jnp.ones((16, 1), BF16)
    first_blk = i * Q_SUBTILES

    weights = []
    q_stats = []
    for sub, head in chains:
        own = first_blk + sub
        qt = qt_ref[:, sub * tq:(sub + 1) * tq]
        w_top = jnp.where(wrow // HEAD_DIM == head, qt, jnp.zeros_like(qt))
        if moba:
            km = side_ref[...].astype(BF16)
            kmh = jnp.where(lax.broadcasted_iota(jnp.int32, km.shape, 1) // HEAD_DIM == head, km, jnp.zeros_like(km))
            blk = (wrow - MOBA_LANE0).astype(F32)
            past = (wrow >= MOBA_LANE0) & (wrow < MOBA_LANE0 + own)
            gate = jnp.where(past, _dot(kmh, qt), NEG_INF)
            sel = jnp.zeros(gate.shape, jnp.bool_)
            for _ in range(MOBA_TOPK):
                best = jnp.max(gate, axis=0, keepdims=True)
                pick = jnp.min(jnp.where(gate == best, blk, float(LANES)), axis=0, keepdims=True)
                chosen = blk == pick
                sel = sel | chosen
                gate = jnp.where(chosen, -jnp.inf, gate)
            w_bot = jnp.where(past & jnp.logical_not(sel), NEG_INF, 0.0)
        else:
            hg = g * HEADS_PER_GROUP + head
            qc = side_ref[head:head + 1, pl.ds(pl.multiple_of(own * tq, tq), tq)]
            hi, mid, lo = _split_bf16(qc)
            qf = w_top.astype(F32)
            q_stats.append((jnp.max(jnp.sum(qf * qf, axis=0, keepdims=True), axis=1, keepdims=True),
                            jnp.max(qc, axis=1, keepdims=True)))
            w_bot = jnp.where((wrow < N_PIECES * n_fox) & (wrow % n_fox == hg), 1.0,
                              jnp.where(wrow == N_PIECES * n_fox, hi,
                                        jnp.where(wrow == N_PIECES * n_fox + 1, mid,
                                                  jnp.where(wrow == N_PIECES * n_fox + 2, lo, 0.0))))
        weights.append(jnp.concatenate([w_top, w_bot.astype(BF16)], axis=0))

    def keys(blk):
        off = pl.multiple_of(blk * tq, tq)
        return jnp.concatenate([k_ref[pl.ds(off, tq), :], kbias_ref[pl.ds(off, tq), :]], axis=1)

    def values(blk, head):
        vh = vt_ref[head * HEAD_DIM:(head + 1) * HEAD_DIM, pl.ds(pl.multiple_of(blk * tq, tq), tq)]
        return jnp.concatenate([vh, jnp.broadcast_to(ones_rows, (16, tq))], axis=0)

    def update(st, blk, head, m, acc):
        m_new = jnp.maximum(m, jnp.max(st, axis=0, keepdims=True))
        p = jnp.exp(st - m_new).astype(BF16)
        return m_new, jnp.exp(m - m_new) * acc + _dot(values(blk, head), p)

    krow = lax.broadcasted_iota(jnp.int32, (tq, tq), 0)
    qcol = lax.broadcasted_iota(jnp.int32, (tq, tq), 1)
    carry = []
    for c, (sub, head) in enumerate(chains):
        own = first_blk + sub
        st = jnp.where(krow <= qcol, _dot(keys(own), weights[c]), NEG_INF)
        m = jnp.max(st, axis=0, keepdims=True)
        p = jnp.exp(st - m).astype(BF16)
        acc = _dot(values(own, head), p)
        for earlier in range(sub):
            blk = first_blk + earlier
            m, acc = update(_dot(keys(blk), weights[c]), blk, head, m, acc)
        carry += [m, acc]

    def produce(blk, buf):
        kaug = keys(blk)
        for c in range(n_chain):
            st = _dot(kaug, weights[c])
            s_bufs[buf][c][...] = st
            mx_bufs[buf][c][...] = jnp.max(st.reshape(tq // 8, 8, tq), axis=0)

    def consume(blk, buf, carry):
        out = []
        for c, (sub, head) in enumerate(chains):
            m, acc = carry[2 * c], carry[2 * c + 1]
            m_new = jnp.maximum(m, jnp.max(mx_bufs[buf][c][...], axis=0, keepdims=True))
            p = jnp.exp(s_bufs[buf][c][...] - m_new).astype(BF16)
            out += [m_new, jnp.exp(m - m_new) * acc + _dot(values(blk, head), p)]
        return out

    def pair(t, carry):
        j = 2 * t
        produce(j + 1, 1)
        carry = consume(j, 0, carry)
        produce(j + 2, 0)
        return consume(j + 1, 1, carry)

    n_past = first_blk
    first_pair = 0
    if not moba:
        blk_lane = lax.broadcasted_iota(jnp.int32, (1, LANES), 1).astype(F32)
        first_needed = jnp.full((1, 1), float(LANES), F32)
        for c, (sub, head) in enumerate(chains):
            qn2, qc_max = q_stats[c]
            kn2 = kstat_ref[2 * head:2 * head + 1, :]
            kc_min = kstat_ref[2 * head + 1:2 * head + 2, :]
            bound = jnp.sqrt(qn2 * kn2) * BOUND_SLACK_MUL + BOUND_SLACK_ADD + (qc_max - kc_min)
            m_low = jnp.min(carry[2 * c], axis=1, keepdims=True)
            needed = jnp.logical_not(bound < m_low - EXP_ZERO_GAP)
            first_needed = jnp.minimum(first_needed,
                                       jnp.min(jnp.where(needed, blk_lane, float(LANES)), axis=1, keepdims=True))
        first_pair = jnp.clip(jnp.min(first_needed).astype(jnp.int32), 0, n_past) // 2
    produce(2 * first_pair, 0)
    carry = lax.fori_loop(first_pair, n_past // 2, pair, carry)
    if Q_SUBTILES % 2:
        carry = lax.cond(n_past % 2 == 1, lambda c: consume(n_past - 1, 0, c), lambda c: c, carry)

    for sub in range(Q_SUBTILES):
        outs = []
        for head in range(HEADS_PER_GROUP):
            acc = carry[2 * (sub * HEADS_PER_GROUP + head) + 1]
            outs.append(acc[:HEAD_DIM] / acc[HEAD_DIM:HEAD_DIM + 1])
        o_ref[sub * tq:(sub + 1) * tq, :] = jnp.concatenate(outs, axis=0).T.astype(o_ref.dtype)


def _prompt_attn(qt, k, vt, kbias, side, kstat=None, *, moba, n_fox):
    bsz, w, l = qt.shape
    tq = Q_SUBTILES * MOBA_BLOCK
    assert l % tq == 0
    groups = w // LANES
    n_chain = Q_SUBTILES * HEADS_PER_GROUP
    if moba:
        kbias_spec = pl.BlockSpec((l, LANES), lambda b, g, i: (0, 0))
        side_spec = pl.BlockSpec((None, LANES, LANES), lambda b, g, i: (b, 0, g))
        extra_specs, extra_args = [], []
    else:
        kbias_spec = pl.BlockSpec((None, l, LANES), lambda b, g, i: (b, 0, 0))
        side = side.reshape(bsz, groups, HEADS_PER_GROUP, l)
        side_spec = pl.BlockSpec((None, None, HEADS_PER_GROUP, l), lambda b, g, i: (b, g, 0, 0))
        n_blocks = kstat.shape[1]
        assert n_blocks <= LANES
        kstat = jnp.transpose(kstat[:, :, :, :groups * HEADS_PER_GROUP], (0, 3, 2, 1))
        kstat = jnp.pad(kstat, ((0, 0), (0, 0), (0, 0), (0, LANES - n_blocks)))
        kstat = kstat.reshape(bsz, groups, 2 * HEADS_PER_GROUP, LANES)
        extra_specs = [pl.BlockSpec((None, None, 2 * HEADS_PER_GROUP, LANES), lambda b, g, i: (b, g, 0, 0))]
        extra_args = [kstat]
    return pl.pallas_call(
        functools.partial(_prompt_attn_kernel, moba=moba, n_fox=n_fox),
        grid=(bsz, groups, l // tq),
        in_specs=[pl.BlockSpec((None, LANES, tq), lambda b, g, i: (b, g, i)),
                  pl.BlockSpec((None, l, LANES), lambda b, g, i: (b, 0, g)),
                  pl.BlockSpec((None, LANES, l), lambda b, g, i: (b, g, 0)),
                  kbias_spec, side_spec, *extra_specs],
        out_specs=pl.BlockSpec((None, tq, LANES), lambda b, g, i: (b, i, g)),
        out_shape=jax.ShapeDtypeStruct((bsz, l, w), BF16),
        scratch_shapes=([pltpu.VMEM((MOBA_BLOCK, MOBA_BLOCK), F32)] * (2 * n_chain)
                        + [pltpu.VMEM((8, MOBA_BLOCK), F32)] * (2 * n_chain)),
        compiler_params=_params("parallel", "parallel", "arbitrary"),
        name="moba_prompt" if moba else "fox_prompt",
    )(qt, k, vt, kbias, side, *extra_args)


PAGES_PER_STEP = 32


def _lane_bcast(col, n):
    return jnp.broadcast_to(col, (col.shape[0], n))


def _page_scores(qb_scr, k_refs):
    rows = []
    for head in range(qb_scr.shape[0]):
        qh = qb_scr[head]
        rows.append(jnp.concatenate([jnp.sum(r[head] * qh, axis=0, keepdims=True) for r in k_refs], axis=1))
    return jnp.concatenate(rows, axis=0)


def _accumulate_values(acc_scr, v_refs, p, alpha):
    page = acc_scr.shape[2]
    for head in range(acc_scr.shape[0]):
        a = acc_scr[head]
        if alpha is not None:
            a = a * alpha[head:head + 1, :]
        for i, r in enumerate(v_refs):
            a = a + r[head] * p[head:head + 1, i * page:(i + 1) * page]
        acc_scr[head] = a


def _finish_rows(acc_scr, l):
    h, hd, page = acc_scr.shape
    parts = [acc_scr[head] / l[head:head + 1, :] for head in range(h)]
    return jnp.sum(jnp.concatenate(parts, axis=0).T, axis=0, keepdims=True)


def _new_token_values(vnt_ref, acc_scr, weight):
    h, hd, page = acc_scr.shape
    lane0 = lax.broadcasted_iota(jnp.int32, (hd, page), 1) == 0
    for head in range(h):
        col = vnt_ref[:, head:head + 1]
        if weight is not None:
            col = col * weight[head:head + 1, :]
        acc_scr[head] = jnp.where(lane0, _lane_bcast(col, page), 0.0)


def _fox_sample_kernel(pt_ref, q_ref, qt_ref, kn_ref, vnt_ref, lfn_ref, *rest):
    pps = PAGES_PER_STEP
    k_refs, v_refs, lf_refs = rest[:pps], rest[pps:2 * pps], rest[2 * pps:3 * pps]
    o_ref, qb_scr, m_scr, l_scr, acc_scr, carry_scr = rest[3 * pps:]
    c = pl.program_id(1)
    page = acc_scr.shape[2]

    @pl.when(c == 0)
    def _():
        for head in range(qb_scr.shape[0]):
            qb_scr[head] = _lane_bcast(qt_ref[:, head:head + 1] * SCALE, page)
        m_scr[...] = jnp.sum(q_ref[...] * SCALE * kn_ref[...], axis=1, keepdims=True)
        l_scr[...] = jnp.ones_like(l_scr)
        _new_token_values(vnt_ref, acc_scr, None)
        carry_scr[...] = lfn_ref[...]

    lf = jnp.concatenate([r[...] for r in lf_refs], axis=1)
    nk = lf.shape[1]
    lane = lax.broadcasted_iota(jnp.int32, lf.shape, 1)
    sfx = jnp.where(lane < nk - 1, pltpu.roll(lf, nk - 1, 1), 0.0)
    d = 1
    while d < nk:
        sfx = sfx + jnp.where(lane < nk - d, pltpu.roll(sfx, nk - d, 1), 0.0)
        d *= 2
    carry = carry_scr[...]
    s = _page_scores(qb_scr, k_refs) + (carry + sfx)
    carry_scr[...] = carry + jnp.sum(lf, axis=1, keepdims=True)
    m = m_scr[...]
    m_new = jnp.maximum(m, jnp.max(s, axis=1, keepdims=True))
    alpha = jnp.exp(m - m_new)
    p = jnp.exp(s - m_new)
    m_scr[...] = m_new
    l_scr[...] = alpha * l_scr[...] + jnp.sum(p, axis=1, keepdims=True)
    _accumulate_values(acc_scr, v_refs, p, alpha)

    @pl.when(c == pl.num_programs(1) - 1)
    def _():
        o_ref[...] = _finish_rows(acc_scr, l_scr[...])


def _sample_row_specs(h, hd):
    nat = pl.BlockSpec((None, h, hd), lambda s, c, pt: (s, 0, 0))
    tr = pl.BlockSpec((None, hd, h), lambda s, c, pt: (s, 0, 0))
    return nat, tr


def _fox_sample(page_table, q, k_new, v_new, lf_new, cache_kt, cache_vt, cache_lft):
    s_n, h, hd = q.shape
    n_pages = page_table.shape[1]
    pps = PAGES_PER_STEP
    assert n_pages % pps == 0
    nc = n_pages // pps

    def page(i, nd):
        return lambda s, c, pt: (pt[s * n_pages + (n_pages - (c + 1) * pps + i)],) + (0,) * nd

    nat, tr = _sample_row_specs(h, hd)
    in_specs = [nat, tr, nat, tr, pl.BlockSpec((None, h, 1), lambda s, c, pt: (s, 0, 0))]
    in_specs += [pl.BlockSpec((None, h, hd, PAGE_SIZE), page(i, 3)) for i in range(pps)]
    in_specs += [pl.BlockSpec((None, h, hd, PAGE_SIZE), page(i, 3)) for i in range(pps)]
    in_specs += [pl.BlockSpec((None, h, PAGE_SIZE), page(i, 2)) for i in range(pps)]
    out = pl.pallas_call(
        _fox_sample_kernel,
        grid_spec=pltpu.PrefetchScalarGridSpec(
            num_scalar_prefetch=1, grid=(s_n, nc), in_specs=in_specs,
            out_specs=pl.BlockSpec((None, 1, h * hd), lambda s, c, pt: (s, 0, 0)),
            scratch_shapes=[pltpu.VMEM((h, hd, PAGE_SIZE), F32), pltpu.VMEM((h, 1), F32), pltpu.VMEM((h, 1), F32),
                            pltpu.VMEM((h, hd, PAGE_SIZE), F32), pltpu.VMEM((h, 1), F32)]),
        out_shape=jax.ShapeDtypeStruct((s_n, 1, h * hd), F32),
        compiler_params=_params("parallel", "arbitrary"),
        name="fox_sample",
    )(page_table.reshape(-1), q, jnp.swapaxes(q, 1, 2), k_new, jnp.swapaxes(v_new, 1, 2),
      lf_new.reshape(s_n, h, 1), *([cache_kt] * pps), *([cache_vt] * pps), *([cache_lft] * pps))
    return out.reshape(s_n, h * hd)


def _moba_sample_kernel(pt_ref, q_ref, qt_ref, kn_ref, vnt_ref, *rest):
    pps = PAGES_PER_STEP
    k_refs, v_refs = rest[:pps], rest[pps:2 * pps]
    o_ref, qb_scr, s_scr, p_scr, l_scr, acc_scr = rest[2 * pps:]
    s = pl.program_id(0)
    c = pl.program_id(1)
    n_seq = pl.num_programs(0) - 1
    nc = s_scr.shape[0]
    nk = s_scr.shape[2]
    page = acc_scr.shape[2]
    has_keys = s < n_seq
    has_values = s >= 1

    @pl.when(has_keys & (c == 0))
    def _():
        for head in range(qb_scr.shape[0]):
            qb_scr[head] = _lane_bcast(qt_ref[:, head:head + 1], page)

    @pl.when(has_keys)
    def _():
        s_scr[c] = _page_scores(qb_scr, k_refs)

    @pl.when(has_values)
    def _():
        _accumulate_values(acc_scr, v_refs, p_scr[c], None)

    @pl.when(has_values & (c == nc - 1))
    def _():
        o_ref[...] = _finish_rows(acc_scr, l_scr[...])

    @pl.when(has_keys & (c == nc - 1))
    def _():
        bpc = nk // MOBA_BLOCK
        gates = []
        for cc in range(nc):
            sc = s_scr[cc]
            for b in range(bpc):
                seg = sc[:, b * MOBA_BLOCK:(b + 1) * MOBA_BLOCK]
                gates.append(jnp.broadcast_to(jnp.sum(seg, axis=1, keepdims=True) * (1.0 / MOBA_BLOCK),
                                              seg.shape))
        gate = jnp.concatenate(gates, axis=1)
        blk = (lax.broadcasted_iota(jnp.int32, gate.shape, 1) // MOBA_BLOCK).astype(F32)
        n_blk = float(gate.shape[1] // MOBA_BLOCK)
        sel = jnp.zeros(gate.shape, jnp.bool_)
        for _ in range(MOBA_TOPK):
            best = jnp.max(gate, axis=1, keepdims=True)
            pick = jnp.min(jnp.where(gate == best, blk, n_blk), axis=1, keepdims=True)
            chosen = blk == pick
            sel = sel | chosen
            gate = jnp.where(chosen, -jnp.inf, gate)
        s_self = jnp.sum(q_ref[...] * kn_ref[...], axis=1, keepdims=True) * SCALE
        s_all = jnp.concatenate([s_scr[cc] for cc in range(nc)], axis=1) * SCALE
        s_all = jnp.where(sel, s_all, NEG_INF)
        m = jnp.maximum(jnp.max(s_all, axis=1, keepdims=True), s_self)
        p = jnp.exp(s_all - m)
        p_self = jnp.exp(s_self - m)
        l_scr[...] = jnp.sum(p, axis=1, keepdims=True) + p_self
        _new_token_values(vnt_ref, acc_scr, p_self)
        for cc in range(nc):
            p_scr[cc] = p[:, cc * nk:(cc + 1) * nk]


def _moba_sample(page_table, q, k_new, v_new, cache_kt, cache_vt):
    s_n, h, hd = q.shape
    n_pages = page_table.shape[1]
    pps = PAGES_PER_STEP
    assert n_pages % pps == 0
    nc = n_pages // pps
    nk = pps * PAGE_SIZE
    assert nk % MOBA_BLOCK == 0 and n_pages * PAGE_SIZE // MOBA_BLOCK >= MOBA_TOPK

    last = s_n - 1

    def kpage(i):
        return lambda s, c, pt: (pt[jnp.minimum(s, last) * n_pages + c * pps + i],) + (0,) * 3

    def vpage(i):
        return lambda s, c, pt: (pt[jnp.maximum(s - 1, 0) * n_pages + c * pps + i],) + (0,) * 3

    cur = lambda shape: pl.BlockSpec((None,) + shape, lambda s, c, pt: (jnp.minimum(s, last), 0, 0))
    in_specs = [cur((h, hd)), cur((hd, h)), cur((h, hd)), cur((hd, h))]
    in_specs += [pl.BlockSpec((None, h, hd, PAGE_SIZE), kpage(i)) for i in range(pps)]
    in_specs += [pl.BlockSpec((None, h, hd, PAGE_SIZE), vpage(i)) for i in range(pps)]
    out = pl.pallas_call(
        _moba_sample_kernel,
        grid_spec=pltpu.PrefetchScalarGridSpec(
            num_scalar_prefetch=1, grid=(s_n + 1, nc), in_specs=in_specs,
            out_specs=pl.BlockSpec((None, 1, h * hd), lambda s, c, pt: (jnp.maximum(s - 1, 0), 0, 0)),
            scratch_shapes=[pltpu.VMEM((h, hd, PAGE_SIZE), F32), pltpu.VMEM((nc, h, nk), F32),
                            pltpu.VMEM((nc, h, nk), F32), pltpu.VMEM((h, 1), F32),
                            pltpu.VMEM((h, hd, PAGE_SIZE), F32)]),
        out_shape=jax.ShapeDtypeStruct((s_n, 1, h * hd), F32),
        compiler_params=_params("arbitrary", "arbitrary"),
        name="moba_sample",
    )(page_table.reshape(-1), q, jnp.swapaxes(q, 1, 2), k_new, jnp.swapaxes(v_new, 1, 2),
      *([cache_kt] * pps), *([cache_vt] * pps))
    return out.reshape(s_n, h * hd)


def _merge_kernel(h_ref, oa_ref, ob_ref, gl_ref, ga_ref, woa_ref, wob_ref, wout_ref, o_ref):
    d = h_ref.shape[1]
    ya = _dot(oa_ref[...].astype(BF16), woa_ref[...])
    yb = _dot(ob_ref[...].astype(BF16), wob_ref[...])
    gl = gl_ref[...]
    merged = jax.nn.sigmoid(gl[:, :d]) * ya + jax.nn.sigmoid(gl[:, d:]) * yb
    o_ref[...] = h_ref[...] + ga_ref[...] * _dot(merged.astype(BF16), wout_ref[...])


def _merge(h, oa, ob, gl, mod, woa, wob, wout, tokens_per_seq):
    t, d = h.shape
    tm = min(t, 512)
    tps = max(tokens_per_seq // tm, 1)
    tok = lambda i: (i, 0)
    const = lambda i: (0, 0)
    return pl.pallas_call(
        _merge_kernel,
        grid=(t // tm,),
        in_specs=[pl.BlockSpec((tm, d), tok), pl.BlockSpec((tm, oa.shape[1]), tok),
                  pl.BlockSpec((tm, ob.shape[1]), tok), pl.BlockSpec((tm, gl.shape[1]), tok),
                  _mod_spec(mod, 5, tm, tps),
                  pl.BlockSpec(woa.shape, const), pl.BlockSpec(wob.shape, const), pl.BlockSpec(wout.shape, const)],
        out_specs=pl.BlockSpec((tm, d), tok),
        out_shape=jax.ShapeDtypeStruct((t, d), F32),
        compiler_params=_params("parallel"),
        name="merge",
    )(h, oa, ob, gl, mod, woa, wob, wout)


def _rope_tables(pos):
    half = HEAD_DIM // 2
    inv = ROPE_THETA ** (-jnp.arange(half, dtype=F32) / half)
    ang = pos.astype(F32)[:, None] * inv[None, :]
    cos, sin = jnp.cos(ang), jnp.sin(ang)
    reps = LANES // HEAD_DIM
    return (jnp.tile(jnp.concatenate([cos, cos], axis=1), (1, reps)),
            jnp.tile(jnp.concatenate([-sin, sin], axis=1), (1, reps)))


def kernel(x_prompt, x_sample, cache_moba_k, cache_moba_v, cache_fox_k, cache_fox_v, cache_fox_logf, page_table,
           c_prompt, c_sample, w_ada, b_ada, g_ff1, w_ff1_in, w_ff1_out, g_mix, w_mix, b_forget, w_o_moba,
           w_o_fox, w_out, g_ff2, w_ff2_in, w_ff2_out, g_final):
    bsz, l, d = x_prompt.shape
    s_n, dec_seq, _ = x_sample.shape
    assert dec_seq == 1
    depth = w_ada.shape[0]
    h_moba, h_fox = cache_moba_k.shape[3], cache_fox_k.shape[3]
    w_moba, w_fox = h_moba * HEAD_DIM, h_fox * HEAD_DIM
    past_len = page_table.shape[1] * PAGE_SIZE
    n_qkv = 3 * w_moba + 3 * w_fox
    n_blocks = l // MOBA_BLOCK
    assert l % MOBA_BLOCK == 0 and MOBA_LANE0 + n_blocks <= LANES

    cos_p, sin_p = _rope_tables(jnp.arange(l, dtype=jnp.int32))
    cos_s, sin_s = _rope_tables(jnp.full((1,), past_len, jnp.int32))
    pad_rows = (-(s_n + bsz)) % 8
    c_all = jnp.concatenate([c_sample, c_prompt, jnp.zeros((pad_rows, d), F32)], axis=0)
    block_lanes = (jnp.arange(LANES, dtype=jnp.int32)[None, :] - MOBA_LANE0
                   == jnp.arange(l, dtype=jnp.int32)[:, None] // MOBA_BLOCK).astype(BF16)

    hp = x_prompt.reshape(bsz * l, d)
    hs = x_sample.reshape(s_n, d)
    st_p = [[] for _ in range(5)]
    st_s = [[] for _ in range(5)]
    for layer in range(depth):
        mod = _modulation(c_all, w_ada[layer], b_ada[layer])
        mod_s = mod[:, :s_n]
        mod_p = mod[:, s_n:s_n + bsz].reshape(N_MOD, bsz, 1, d)
        w1i, w1o = w_ff1_in[layer].astype(BF16), w_ff1_out[layer].astype(BF16)
        w2i, w2o = w_ff2_in[layer].astype(BF16), w_ff2_out[layer].astype(BF16)
        wm = w_mix[layer]
        wqkv = wm[:, :n_qkv].astype(BF16)
        wf = jnp.pad(wm[:, n_qkv:n_qkv + h_fox], ((0, 0), (0, LANES - h_fox))).astype(BF16)
        wg = wm[:, n_qkv + h_fox:].astype(BF16)
        bfor = jnp.pad(b_forget[layer], (0, LANES - h_fox)).reshape(1, LANES)
        woa, wob, wout = w_o_moba[layer].astype(BF16), w_o_fox[layer].astype(BF16), w_out[layer].astype(BF16)
        last = layer == depth - 1
        mix_kw = dict(w_moba=w_moba, w_fox=w_fox, n_fox=h_fox)

        hp = _ffn(hp, mod_p, 0, g_ff1[layer], w1i, w1o, l)
        (kat, vat, kbt, vbt, lft, gl, ka16, kb16, qat16, vat16, qbt16, vbt16, kbias, cumt, km, kstat) = _mix(
            hp, mod_p, g_mix[layer], cos_p, sin_p, wqkv, wf, bfor, wg, seq_shape=(bsz, l), **mix_kw)
        km = jnp.pad(km.reshape(bsz, n_blocks, w_moba), ((0, 0), (MOBA_LANE0, LANES - MOBA_LANE0 - n_blocks), (0, 0)))
        oa = _prompt_attn(qat16, ka16.reshape(bsz, l, w_moba), vat16, block_lanes, km, moba=True, n_fox=h_fox)
        ob = _prompt_attn(qbt16, kb16.reshape(bsz, l, w_fox), vbt16, kbias.reshape(bsz, l, LANES), cumt, kstat,
                          moba=False, n_fox=h_fox)
        hp = _merge(hp, oa.reshape(bsz * l, w_moba), ob.reshape(bsz * l, w_fox), gl, mod_p, woa, wob, wout, l)
        hp = _ffn(hp, mod_p, 6, g_ff2[layer], w2i, w2o, l, g_final=g_final if last else None)
        heads = lambda a, h: jnp.transpose(a.reshape(bsz, h, HEAD_DIM, l), (0, 3, 1, 2))
        for i, a in enumerate((heads(kat, h_moba), heads(vat, h_moba), heads(kbt, h_fox), heads(vbt, h_fox),
                               jnp.swapaxes(lft, 1, 2))):
            st_p[i].append(a)

        hs = _ffn(hs, mod_s, 0, g_ff1[layer], w1i, w1o, 1)
        ka, va, kb, vb, lf, gl, qa, qb = _mix(hs, mod_s, g_mix[layer], cos_s, sin_s, wqkv, wf, bfor, wg,
                                              seq_shape=None, **mix_kw)
        ka, va, qa = (a.reshape(s_n, h_moba, HEAD_DIM) for a in (ka, va, qa))
        kb, vb, qb = (a.reshape(s_n, h_fox, HEAD_DIM) for a in (kb, vb, qb))
        pages_t = lambda a: jnp.transpose(a[layer], (0, 2, 3, 1))
        oa = _moba_sample(page_table, qa, ka, va, pages_t(cache_moba_k), pages_t(cache_moba_v))
        ob = _fox_sample(page_table, qb, kb, vb, lf, pages_t(cache_fox_k), pages_t(cache_fox_v),
                         jnp.swapaxes(cache_fox_logf[layer], 1, 2))
        hs = _merge(hs, oa, ob, gl, mod_s, woa, wob, wout, 1)
        hs = _ffn(hs, mod_s, 6, g_ff2[layer], w2i, w2o, 1, g_final=g_final if last else None)
        for i, a in enumerate((ka.reshape(s_n, 1, h_moba, HEAD_DIM), va.reshape(s_n, 1, h_moba, HEAD_DIM),
                               kb.reshape(s_n, 1, h_fox, HEAD_DIM), vb.reshape(s_n, 1, h_fox, HEAD_DIM),
                               lf.reshape(s_n, 1, h_fox))):
            st_s[i].append(a)

    return (hp.reshape(bsz, l, d), hs.reshape(s_n, 1, d),
            *(jnp.stack(a) for a in st_p), *(jnp.stack(a) for a in st_s))
```

```python
import functools
import math

import jax
import jax.numpy as jnp
from jax import lax
from jax.experimental import pallas as pl
from jax.experimental.pallas import tpu as pltpu

HEAD_DIM = 64
MOBA_BLOCK = 256
MOBA_TOPK = 3
PAGE_SIZE = 128
ROPE_THETA = 10000.0
NORM_EPS = 1e-6
NEG_INF = -1e30
N_MOD = 9
LANES = 128
HEADS_PER_GROUP = LANES // HEAD_DIM
VMEM_LIMIT = 56 * 1024 * 1024

F32 = jnp.float32
BF16 = jnp.bfloat16
SCALE = 1.0 / math.sqrt(HEAD_DIM)

N_PIECES = 3
MOBA_LANE0 = 32
Q_SUBTILES = 2
EXP_ZERO_GAP = 110.0
BOUND_SLACK_MUL = 1.02
BOUND_SLACK_ADD = 0.01


def _params(*sem):
    return pltpu.CompilerParams(dimension_semantics=sem, vmem_limit_bytes=VMEM_LIMIT)


def _dot(a, b):
    return jnp.dot(a, b, preferred_element_type=F32)


def _dot_nt(a, b):
    return lax.dot_general(a, b, (((1,), (1,)), ((), ())), preferred_element_type=F32)


def _rms(x, g):
    return x * lax.rsqrt(jnp.mean(x * x, axis=-1, keepdims=True) + NORM_EPS) * g


def _silu(x):
    return x * jax.nn.sigmoid(x)


def _split_bf16(x):
    hi = x.astype(BF16).astype(F32)
    r = x - hi
    mid = r.astype(BF16).astype(F32)
    lo = (r - mid).astype(BF16).astype(F32)
    return hi, mid, lo


def _mod_kernel(c_ref, w_ref, b_ref, o_ref):
    a = _silu(c_ref[...]).astype(BF16)
    o_ref[...] = _dot(a, w_ref[...].astype(BF16)) + b_ref[...]


def _modulation(c, w_ada, b_ada):
    r, d = c.shape
    return pl.pallas_call(
        _mod_kernel,
        grid=(N_MOD,),
        in_specs=[pl.BlockSpec((r, d), lambda j: (0, 0)),
                  pl.BlockSpec((d, d), lambda j: (0, j)),
                  pl.BlockSpec((1, d), lambda j: (0, j))],
        out_specs=pl.BlockSpec((None, r, d), lambda j: (j, 0, 0)),
        out_shape=jax.ShapeDtypeStruct((N_MOD, r, d), F32),
        compiler_params=_params("arbitrary"),
        name="mod",
    )(c, w_ada, b_ada.reshape(1, -1))


def _mod_spec(mod, k, tm, tiles_per_seq):
    d = mod.shape[-1]
    if mod.ndim == 4:
        return pl.BlockSpec((None, None, 1, d), lambda i, *_: (k, i // tiles_per_seq, 0, 0))
    return pl.BlockSpec((None, tm, d), lambda i, *_: (k, i, 0))


def _ffn_kernel(x_ref, sh_ref, sc_ref, ga_ref, g_ref, wa_ref, wb_ref, wo_ref, *rest, final_norm):
    if final_norm:
        gf_ref, o_ref, n_scr, acc_scr = rest
    else:
        o_ref, n_scr, acc_scr = rest
    c = pl.program_id(1)

    @pl.when(c == 0)
    def _():
        n = _rms(x_ref[...], g_ref[...]) * (1 + sc_ref[...]) + sh_ref[...]
        n_scr[...] = n.astype(BF16)
        acc_scr[...] = jnp.zeros_like(acc_scr)

    n = n_scr[...]
    a = _dot(n, wa_ref[...])
    b = _dot(n, wb_ref[...])
    acc_scr[...] += _dot((_silu(a) * b).astype(BF16), wo_ref[...])

    @pl.when(c == pl.num_programs(1) - 1)
    def _():
        h = x_ref[...] + 0.5 * ga_ref[...] * acc_scr[...]
        if final_norm:
            h = _rms(h, gf_ref[...])
        o_ref[...] = h


def _ffn_tiles(t, d_ff):
    tm = min(t, 1024)
    tf = d_ff
    for parts in (4, 2):
        if d_ff % (parts * LANES) == 0:
            tf = d_ff // parts
            break
    return tm, tf


def _ffn(x, mod, k0, g, w_in, w_out, tiles_per_seq_tokens, g_final=None):
    t, d = x.shape
    d_ff = w_out.shape[0]
    tm, tf = _ffn_tiles(t, d_ff)
    nc = d_ff // tf
    tps = max(tiles_per_seq_tokens // tm, 1)
    in_specs = [pl.BlockSpec((tm, d), lambda i, c: (i, 0)),
                _mod_spec(mod, k0, tm, tps), _mod_spec(mod, k0 + 1, tm, tps), _mod_spec(mod, k0 + 2, tm, tps),
                pl.BlockSpec((1, d), lambda i, c: (0, 0)),
                pl.BlockSpec((d, tf), lambda i, c: (0, c)),
                pl.BlockSpec((d, tf), lambda i, c: (0, c + nc)),
                pl.BlockSpec((tf, d), lambda i, c: (c, 0))]
    args = [x, mod, mod, mod, g.reshape(1, d), w_in, w_in, w_out]
    if g_final is not None:
        in_specs.append(pl.BlockSpec((1, d), lambda i, c: (0, 0)))
        args.append(g_final.reshape(1, d))
    return pl.pallas_call(
        functools.partial(_ffn_kernel, final_norm=g_final is not None),
        grid=(t // tm, nc),
        in_specs=in_specs,
        out_specs=pl.BlockSpec((tm, d), lambda i, c: (i, 0)),
        out_shape=jax.ShapeDtypeStruct((t, d), F32),
        scratch_shapes=[pltpu.VMEM((tm, d), BF16), pltpu.VMEM((tm, d), F32)],
        compiler_params=_params("parallel", "arbitrary"),
        name="ffn_final" if g_final is not None else "ffn",
    )(*args)


def _rope(x, cos, sin_signed):
    w = x.shape[1]
    reps = w // LANES
    cos = jnp.concatenate([cos] * reps, axis=1)
    sin_signed = jnp.concatenate([sin_signed] * reps, axis=1)
    lane = lax.broadcasted_iota(jnp.int32, x.shape, 1)
    first_half = (lane % HEAD_DIM) < (HEAD_DIM // 2)
    swapped = jnp.where(first_half, pltpu.roll(x, w - HEAD_DIM // 2, 1), pltpu.roll(x, HEAD_DIM // 2, 1))
    return x * cos + swapped * sin_signed


def _mix_kernel(h_ref, sh_ref, sc_ref, g_ref, cos_ref, sin_ref, wqkv_ref, wf_ref, bf_ref, wg_ref, *rest,
                w_moba, w_fox, n_fox, seq):
    if seq:
        (kat_ref, vat_ref, kbt_ref, vbt_ref, lft_ref, gl_ref, ka16_ref, kb16_ref, qat16_ref, vat16_ref, qbt16_ref,
         vbt16_ref, kbias_ref, cumt_ref, km_ref, kstat_ref, carry_scr) = rest
    else:
        ka_ref, va_ref, kb_ref, vb_ref, lf_ref, gl_ref, qa_ref, qb_ref = rest
    n = (_rms(h_ref[...], g_ref[...]) * (1 + sc_ref[...]) + sh_ref[...]).astype(BF16)
    z = _dot(n, wqkv_ref[...])
    o = 0
    qa = _rope(z[:, o:o + w_moba], cos_ref[...], sin_ref[...]); o += w_moba
    ka = _rope(z[:, o:o + w_moba], cos_ref[...], sin_ref[...]); o += w_moba
    va = z[:, o:o + w_moba]; o += w_moba
    qb = z[:, o:o + w_fox]; o += w_fox
    kb = z[:, o:o + w_fox]; o += w_fox
    vb = z[:, o:o + w_fox]
    lf = jax.nn.log_sigmoid(_dot(n, wf_ref[...]) + bf_ref[...])
    gl_ref[...] = _dot(n, wg_ref[...])
    if not seq:
        ka_ref[...] = ka
        va_ref[...] = va
        kb_ref[...] = kb
        vb_ref[...] = vb
        lf_ref[...] = lf[:, :n_fox]
        qa_ref[...] = qa
        qb_ref[...] = qb
        return
    kat, vat, kbt, vbt = ka.T, va.T, kb.T, vb.T
    kat_ref[...] = kat
    vat_ref[...] = vat
    kbt_ref[...] = kbt
    vbt_ref[...] = vbt
    lft_ref[...] = lf.T[:n_fox, :]
    ka16_ref[...] = ka.astype(BF16)
    kb16_ref[...] = kb.astype(BF16)
    qat16_ref[...] = (qa * SCALE).T.astype(BF16)
    vat16_ref[...] = vat.astype(BF16)
    qbt16_ref[...] = (qb * SCALE).T.astype(BF16)
    vbt16_ref[...] = vbt.astype(BF16)
    tm = lf.shape[0]

    @pl.when(pl.program_id(1) == 0)
    def _():
        carry_scr[...] = jnp.zeros_like(carry_scr)

    row = lax.broadcasted_iota(jnp.int32, (tm, tm), 0)
    col = lax.broadcasted_iota(jnp.int32, (tm, tm), 1)
    tri = (row >= col).astype(F32)
    cum = jnp.dot(tri, lf, precision=lax.Precision.HIGHEST, preferred_element_type=F32) + carry_scr[...]
    carry_scr[...] = cum[tm - 1:tm, :]
    cumt_ref[...] = cum.T[:n_fox, :]
    hi, mid, lo = _split_bf16(cum)
    lane = lax.broadcasted_iota(jnp.int32, cum.shape, 1)
    kbias = jnp.where(lane < n_fox, -hi,
                      jnp.where(lane < 2 * n_fox, -pltpu.roll(mid, n_fox, 1),
                                jnp.where(lane < 3 * n_fox, -pltpu.roll(lo, 2 * n_fox, 1),
                                          jnp.where(lane < 3 * n_fox + N_PIECES, 1.0, 0.0))))
    kbias_ref[...] = kbias.astype(BF16)
    km_ref[...] = jnp.sum(ka, axis=0, keepdims=True) * (1.0 / MOBA_BLOCK)
    head_of = lax.broadcasted_iota(jnp.int32, (w_fox, LANES), 0) // HEAD_DIM
    seg = (head_of == lax.broadcasted_iota(jnp.int32, (w_fox, LANES), 1)).astype(BF16)
    norm2 = jnp.max(_dot((kb * kb).astype(BF16), seg), axis=0, keepdims=True)
    kstat_ref[...] = jnp.concatenate([norm2, jnp.min(cum, axis=0, keepdims=True)], axis=0)


def _mix(h, mod, g, cos, sin_signed, wqkv, wf, bf, wg, *, w_moba, w_fox, n_fox, seq_shape):
    t, d = h.shape
    seq = seq_shape is not None
    tm = min(t, MOBA_BLOCK)
    if seq:
        bsz, l = seq_shape
        nt = l // tm
        grid = (bsz, nt)
        tok = lambda b, i: (b * nt + i, 0)
        rope_spec = pl.BlockSpec((tm, LANES), lambda b, i: (i, 0))
        mod_specs = [pl.BlockSpec((None, None, 1, d), functools.partial(lambda b, i, k: (k, b, 0, 0), k=k))
                     for k in (3, 4)]
        sem = ("parallel", "arbitrary")
    else:
        grid = (t // tm,)
        tok = lambda i: (i, 0)
        rope_spec = pl.BlockSpec((1, LANES), lambda i: (0, 0))
        mod_specs = [pl.BlockSpec((None, tm, d), functools.partial(lambda i, k: (k, i, 0), k=k)) for k in (3, 4)]
        sem = ("parallel",)
    const = lambda *_: (0, 0)
    in_specs = [pl.BlockSpec((tm, d), tok), *mod_specs,
                pl.BlockSpec((1, d), const), rope_spec, rope_spec,
                pl.BlockSpec(wqkv.shape, const), pl.BlockSpec(wf.shape, const), pl.BlockSpec(bf.shape, const),
                pl.BlockSpec(wg.shape, const)]
    scratch = []
    if seq:
        assert tm == MOBA_BLOCK and 3 * n_fox + N_PIECES <= LANES
        tr = lambda b, i: (b, 0, i)
        out_specs = [pl.BlockSpec((None, w_moba, tm), tr), pl.BlockSpec((None, w_moba, tm), tr),
                     pl.BlockSpec((None, w_fox, tm), tr), pl.BlockSpec((None, w_fox, tm), tr),
                     pl.BlockSpec((None, n_fox, tm), tr), pl.BlockSpec((tm, wg.shape[1]), tok),
                     pl.BlockSpec((tm, w_moba), tok), pl.BlockSpec((tm, w_fox), tok),
                     pl.BlockSpec((None, w_moba, tm), tr), pl.BlockSpec((None, w_moba, tm), tr),
                     pl.BlockSpec((None, w_fox, tm), tr), pl.BlockSpec((None, w_fox, tm), tr),
                     pl.BlockSpec((tm, LANES), tok),
                     pl.BlockSpec((None, n_fox, tm), tr),
                     pl.BlockSpec((None, None, 1, w_moba), lambda b, i: (b, i, 0, 0)),
                     pl.BlockSpec((None, None, 2, LANES), lambda b, i: (b, i, 0, 0))]
        out_shape = [jax.ShapeDtypeStruct((bsz, w_moba, l), F32), jax.ShapeDtypeStruct((bsz, w_moba, l), F32),
                     jax.ShapeDtypeStruct((bsz, w_fox, l), F32), jax.ShapeDtypeStruct((bsz, w_fox, l), F32),
                     jax.ShapeDtypeStruct((bsz, n_fox, l), F32), jax.ShapeDtypeStruct((t, wg.shape[1]), F32),
                     jax.ShapeDtypeStruct((t, w_moba), BF16), jax.ShapeDtypeStruct((t, w_fox), BF16),
                     jax.ShapeDtypeStruct((bsz, w_moba, l), BF16), jax.ShapeDtypeStruct((bsz, w_moba, l), BF16),
                     jax.ShapeDtypeStruct((bsz, w_fox, l), BF16), jax.ShapeDtypeStruct((bsz, w_fox, l), BF16),
                     jax.ShapeDtypeStruct((t, LANES), BF16),
                     jax.ShapeDtypeStruct((bsz, n_fox, l), F32),
                     jax.ShapeDtypeStruct((bsz, nt, 1, w_moba), F32),
                     jax.ShapeDtypeStruct((bsz, nt, 2, LANES), F32)]
        scratch = [pltpu.VMEM((1, LANES), F32)]
    else:
        out_specs = [pl.BlockSpec((tm, w_moba), tok), pl.BlockSpec((tm, w_moba), tok),
                     pl.BlockSpec((tm, w_fox), tok), pl.BlockSpec((tm, w_fox), tok),
                     pl.BlockSpec((tm, n_fox), tok), pl.BlockSpec((tm, wg.shape[1]), tok),
                     pl.BlockSpec((tm, w_moba), tok), pl.BlockSpec((tm, w_fox), tok)]
        out_shape = [jax.ShapeDtypeStruct((t, w_moba), F32), jax.ShapeDtypeStruct((t, w_moba), F32),
                     jax.ShapeDtypeStruct((t, w_fox), F32), jax.ShapeDtypeStruct((t, w_fox), F32),
                     jax.ShapeDtypeStruct((t, n_fox), F32), jax.ShapeDtypeStruct((t, wg.shape[1]), F32),
                     jax.ShapeDtypeStruct((t, w_moba), F32), jax.ShapeDtypeStruct((t, w_fox), F32)]
    return pl.pallas_call(
        functools.partial(_mix_kernel, w_moba=w_moba, w_fox=w_fox, n_fox=n_fox, seq=seq),
        grid=grid, in_specs=in_specs, out_specs=out_specs, out_shape=out_shape, scratch_shapes=scratch,
        compiler_params=_params(*sem),
        name="mix_seq" if seq else "mix_tok",
    )(h, mod, mod, g.reshape(1, d), cos, sin_signed, wqkv, wf, bf, wg)


def _prompt_attn_kernel(qt_ref, k_ref, vt_ref, kbias_ref, side_ref, *rest, moba, n_fox):
    if moba:
        o_ref, *scratch = rest
    else:
        kstat_ref, o_ref, *scratch = rest
    g = pl.program_id(1)
    i = pl.program_id(2)
    tq = MOBA_BLOCK
    chains = [(sub, head) for sub in range(Q_SUBTILES) for head in range(HEADS_PER_GROUP)]
    n_chain = len(chains)
    s_bufs = [scratch[b * n_chain:(b + 1) * n_chain] for b in range(2)]
    mx_bufs = [scratch[(2 + b) * n_chain:(3 + b) * n_chain] for b in range(2)]
    wrow = lax.broadcasted_iota(jnp.int32, (LANES, tq), 0)
    ones_rows = jnp.ones((16, 1), BF16)
    first_blk = i * Q_SUBTILES

    weights = []
    q_stats = []
    for sub, head in chains:
        own = first_blk + sub
        qt = qt_ref[:, sub * tq:(sub + 1) * tq]
        w_top = jnp.where(wrow // HEAD_DIM == head, qt, jnp.zeros_like(qt))
        if moba:
            km = side_ref[...].astype(BF16)
            kmh = jnp.where(lax.broadcasted_iota(jnp.int32, km.shape, 1) // HEAD_DIM == head, km, jnp.zeros_like(km))
            blk = (wrow - MOBA_LANE0).astype(F32)
            past = (wrow >= MOBA_LANE0) & (wrow < MOBA_LANE0 + own)
            gate = jnp.where(past, _dot(kmh, qt), NEG_INF)
            sel = jnp.zeros(gate.shape, jnp.bool_)
            for _ in range(MOBA_TOPK):
                best = jnp.max(gate, axis=0, keepdims=True)
                pick = jnp.min(jnp.where(gate == best, blk, float(LANES)), axis=0, keepdims=True)
                chosen = blk == pick
                sel = sel | chosen
                gate = jnp.where(chosen, -jnp.inf, gate)
            w_bot = jnp.where(past & jnp.logical_not(sel), NEG_INF, 0.0)
        else:
            hg = g * HEADS_PER_GROUP + head
            qc = side_ref[head:head + 1, pl.ds(pl.multiple_of(own * tq, tq), tq)]
            hi, mid, lo = _split_bf16(qc)
            qf = w_top.astype(F32)
            q_stats.append((jnp.max(jnp.sum(qf * qf, axis=0, keepdims=True), axis=1, keepdims=True),
                            jnp.max(qc, axis=1, keepdims=True)))
            w_bot = jnp.where((wrow < N_PIECES * n_fox) & (wrow % n_fox == hg), 1.0,
                              jnp.where(wrow == N_PIECES * n_fox, hi,
                                        jnp.where(wrow == N_PIECES * n_fox + 1, mid,
                                                  jnp.where(wrow == N_PIECES * n_fox + 2, lo, 0.0))))
        weights.append(jnp.concatenate([w_top, w_bot.astype(BF16)], axis=0))

    def keys(blk):
        off = pl.multiple_of(blk * tq, tq)
        return jnp.concatenate([k_ref[pl.ds(off, tq), :], kbias_ref[pl.ds(off, tq), :]], axis=1)

    def values(blk, head):
        vh = vt_ref[head * HEAD_DIM:(head + 1) * HEAD_DIM, pl.ds(pl.multiple_of(blk * tq, tq), tq)]
        return jnp.concatenate([vh, jnp.broadcast_to(ones_rows, (16, tq))], axis=0)

    def update(st, blk, head, m, acc):
        m_new = jnp.maximum(m, jnp.max(st, axis=0, keepdims=True))
        p = jnp.exp(st - m_new).astype(BF16)
        return m_new, jnp.exp(m - m_new) * acc + _dot(values(blk, head), p)

    krow = lax.broadcasted_iota(jnp.int32, (tq, tq), 0)
    qcol = lax.broadcasted_iota(jnp.int32, (tq, tq), 1)
    for c, (sub, head) in enumerate(chains):
        st = jnp.where(krow <= qcol, _dot(keys(first_blk + sub), weights[c]), NEG_INF)
        s_bufs[1][c][...] = st
        mx_bufs[1][c][...] = jnp.max(st.reshape(tq // 8, 8, tq), axis=0)
    carry = []
    for c, (sub, head) in enumerate(chains):
        own = first_blk + sub
        m = jnp.max(mx_bufs[1][c][...], axis=0, keepdims=True)
        p = jnp.exp(s_bufs[1][c][...] - m).astype(BF16)
        acc = _dot(values(own, head), p)
        for earlier in range(sub):
            blk = first_blk + earlier
            m, acc = update(_dot(keys(blk), weights[c]), blk, head, m, acc)
        carry += [m, acc]

    def produce(blk, buf):
        kaug = keys(blk)
        for c in range(n_chain):
            st = _dot(kaug, weights[c])
            s_bufs[buf][c][...] = st
            mx_bufs[buf][c][...] = jnp.max(st.reshape(tq // 8, 8, tq), axis=0)

    def consume(blk, buf, carry):
        out = []
        for c, (sub, head) in enumerate(chains):
            m, acc = carry[2 * c], carry[2 * c + 1]
            m_new = jnp.maximum(m, jnp.max(mx_bufs[buf][c][...], axis=0, keepdims=True))
            p = jnp.exp(s_bufs[buf][c][...] - m_new).astype(BF16)
            out += [m_new, jnp.exp(m - m_new) * acc + _dot(values(blk, head), p)]
        return out

    def pair(t, carry):
        j = 2 * t
        produce(j + 1, 1)
        carry = consume(j, 0, carry)
        produce(j + 2, 0)
        return consume(j + 1, 1, carry)

    n_past = first_blk
    first_pair = 0
    if not moba:
        blk_lane = lax.broadcasted_iota(jnp.int32, (1, LANES), 1).astype(F32)
        first_needed = jnp.full((1, 1), float(LANES), F32)
        for c, (sub, head) in enumerate(chains):
            qn2, qc_max = q_stats[c]
            kn2 = kstat_ref[2 * head:2 * head + 1, :]
            kc_min = kstat_ref[2 * head + 1:2 * head + 2, :]
            bound = jnp.sqrt(qn2 * kn2) * BOUND_SLACK_MUL + BOUND_SLACK_ADD + (qc_max - kc_min)
            m_low = jnp.min(carry[2 * c], axis=1, keepdims=True)
            needed = jnp.logical_not(bound < m_low - EXP_ZERO_GAP)
            first_needed = jnp.minimum(first_needed,
                                       jnp.min(jnp.where(needed, blk_lane, float(LANES)), axis=1, keepdims=True))
        first_pair = jnp.clip(jnp.min(first_needed).astype(jnp.int32), 0, n_past) // 2
    produce(2 * first_pair, 0)
    carry = lax.fori_loop(first_pair, n_past // 2, pair, carry)
    if Q_SUBTILES % 2:
        carry = lax.cond(n_past % 2 == 1, lambda c: consume(n_past - 1, 0, c), lambda c: c, carry)

    for sub in range(Q_SUBTILES):
        outs = []
        for head in range(HEADS_PER_GROUP):
            acc = carry[2 * (sub * HEADS_PER_GROUP + head) + 1]
            outs.append(acc[:HEAD_DIM] / acc[HEAD_DIM:HEAD_DIM + 1])
        o_ref[sub * tq:(sub + 1) * tq, :] = jnp.concatenate(outs, axis=0).T.astype(o_ref.dtype)


def _prompt_attn(qt, k, vt, kbias, side, kstat=None, *, moba, n_fox):
    bsz, w, l = qt.shape
    tq = Q_SUBTILES * MOBA_BLOCK
    assert l % tq == 0
    groups = w // LANES
    n_chain = Q_SUBTILES * HEADS_PER_GROUP
    if moba:
        kbias_spec = pl.BlockSpec((l, LANES), lambda b, g, i: (0, 0))
        side_spec = pl.BlockSpec((None, LANES, LANES), lambda b, g, i: (b, 0, g))
        extra_specs, extra_args = [], []
    else:
        kbias_spec = pl.BlockSpec((None, l, LANES), lambda b, g, i: (b, 0, 0))
        side = side.reshape(bsz, groups, HEADS_PER_GROUP, l)
        side_spec = pl.BlockSpec((None, None, HEADS_PER_GROUP, l), lambda b, g, i: (b, g, 0, 0))
        n_blocks = kstat.shape[1]
        assert n_blocks <= LANES
        kstat = jnp.transpose(kstat[:, :, :, :groups * HEADS_PER_GROUP], (0, 3, 2, 1))
        kstat = jnp.pad(kstat, ((0, 0), (0, 0), (0, 0), (0, LANES - n_blocks)))
        kstat = kstat.reshape(bsz, groups, 2 * HEADS_PER_GROUP, LANES)
        extra_specs = [pl.BlockSpec((None, None, 2 * HEADS_PER_GROUP, LANES), lambda b, g, i: (b, g, 0, 0))]
        extra_args = [kstat]
    return pl.pallas_call(
        functools.partial(_prompt_attn_kernel, moba=moba, n_fox=n_fox),
        grid=(bsz, groups, l // tq),
        in_specs=[pl.BlockSpec((None, LANES, tq), lambda b, g, i: (b, g, i)),
                  pl.BlockSpec((None, l, LANES), lambda b, g, i: (b, 0, g)),
                  pl.BlockSpec((None, LANES, l), lambda b, g, i: (b, g, 0)),
                  kbias_spec, side_spec, *extra_specs],
        out_specs=pl.BlockSpec((None, tq, LANES), lambda b, g, i: (b, i, g)),
        out_shape=jax.ShapeDtypeStruct((bsz, l, w), BF16),
        scratch_shapes=([pltpu.VMEM((MOBA_BLOCK, MOBA_BLOCK), F32)] * (2 * n_chain)
                        + [pltpu.VMEM((8, MOBA_BLOCK), F32)] * (2 * n_chain)),
        compiler_params=_params("parallel", "parallel", "arbitrary"),
        name="moba_prompt" if moba else "fox_prompt",
    )(qt, k, vt, kbias, side, *extra_args)


PAGES_PER_STEP = 32


def _lane_bcast(col, n):
    return jnp.broadcast_to(col, (col.shape[0], n))


def _page_scores(qb_scr, k_refs):
    rows = []
    for head in range(qb_scr.shape[0]):
        qh = qb_scr[head]
        rows.append(jnp.concatenate([jnp.sum(r[head] * qh, axis=0, keepdims=True) for r in k_refs], axis=1))
    return jnp.concatenate(rows, axis=0)


def _accumulate_values(acc_scr, v_refs, p, alpha):
    page = acc_scr.shape[2]
    for head in range(acc_scr.shape[0]):
        a = acc_scr[head]
        if alpha is not None:
            a = a * alpha[head:head + 1, :]
        for i, r in enumerate(v_refs):
            a = a + r[head] * p[head:head + 1, i * page:(i + 1) * page]
        acc_scr[head] = a


def _finish_rows(acc_scr, l):
    h, hd, page = acc_scr.shape
    parts = [acc_scr[head] / l[head:head + 1, :] for head in range(h)]
    return jnp.sum(jnp.concatenate(parts, axis=0).T, axis=0, keepdims=True)


def _new_token_values(vnt_ref, acc_scr, weight):
    h, hd, page = acc_scr.shape
    lane0 = lax.broadcasted_iota(jnp.int32, (hd, page), 1) == 0
    for head in range(h):
        col = vnt_ref[:, head:head + 1]
        if weight is not None:
            col = col * weight[head:head + 1, :]
        acc_scr[head] = jnp.where(lane0, _lane_bcast(col, page), 0.0)


def _fox_sample_kernel(pt_ref, q_ref, qt_ref, kn_ref, vnt_ref, lfn_ref, *rest):
    pps = PAGES_PER_STEP
    k_refs, v_refs, lf_refs = rest[:pps], rest[pps:2 * pps], rest[2 * pps:3 * pps]
    o_ref, qb_scr, m_scr, l_scr, acc_scr, carry_scr = rest[3 * pps:]
    c = pl.program_id(1)
    page = acc_scr.shape[2]

    @pl.when(c == 0)
    def _():
        for head in range(qb_scr.shape[0]):
            qb_scr[head] = _lane_bcast(qt_ref[:, head:head + 1] * SCALE, page)
        m_scr[...] = jnp.sum(q_ref[...] * SCALE * kn_ref[...], axis=1, keepdims=True)
        l_scr[...] = jnp.ones_like(l_scr)
        _new_token_values(vnt_ref, acc_scr, None)
        carry_scr[...] = lfn_ref[...]

    lf = jnp.concatenate([r[...] for r in lf_refs], axis=1)
    nk = lf.shape[1]
    lane = lax.broadcasted_iota(jnp.int32, lf.shape, 1)
    sfx = jnp.where(lane < nk - 1, pltpu.roll(lf, nk - 1, 1), 0.0)
    d = 1
    while d < nk:
        sfx = sfx + jnp.where(lane < nk - d, pltpu.roll(sfx, nk - d, 1), 0.0)
        d *= 2
    carry = carry_scr[...]
    s = _page_scores(qb_scr, k_refs) + (carry + sfx)
    carry_scr[...] = carry + jnp.sum(lf, axis=1, keepdims=True)
    m = m_scr[...]
    m_new = jnp.maximum(m, jnp.max(s, axis=1, keepdims=True))
    alpha = jnp.exp(m - m_new)
    p = jnp.exp(s - m_new)
    m_scr[...] = m_new
    l_scr[...] = alpha * l_scr[...] + jnp.sum(p, axis=1, keepdims=True)
    _accumulate_values(acc_scr, v_refs, p, alpha)

    @pl.when(c == pl.num_programs(1) - 1)
    def _():
        o_ref[...] = _finish_rows(acc_scr, l_scr[...])


def _sample_row_specs(h, hd):
    nat = pl.BlockSpec((None, h, hd), lambda s, c, pt: (s, 0, 0))
    tr = pl.BlockSpec((None, hd, h), lambda s, c, pt: (s, 0, 0))
    return nat, tr


def _fox_sample(page_table, q, k_new, v_new, lf_new, cache_kt, cache_vt, cache_lft):
    s_n, h, hd = q.shape
    n_pages = page_table.shape[1]
    pps = PAGES_PER_STEP
    assert n_pages % pps == 0
    nc = n_pages // pps

    def page(i, nd):
        return lambda s, c, pt: (pt[s * n_pages + (n_pages - (c + 1) * pps + i)],) + (0,) * nd

    nat, tr = _sample_row_specs(h, hd)
    in_specs = [nat, tr, nat, tr, pl.BlockSpec((None, h, 1), lambda s, c, pt: (s, 0, 0))]
    in_specs += [pl.BlockSpec((None, h, hd, PAGE_SIZE), page(i, 3)) for i in range(pps)]
    in_specs += [pl.BlockSpec((None, h, hd, PAGE_SIZE), page(i, 3)) for i in range(pps)]
    in_specs += [pl.BlockSpec((None, h, PAGE_SIZE), page(i, 2)) for i in range(pps)]
    out = pl.pallas_call(
        _fox_sample_kernel,
        grid_spec=pltpu.PrefetchScalarGridSpec(
            num_scalar_prefetch=1, grid=(s_n, nc), in_specs=in_specs,
            out_specs=pl.BlockSpec((None, 1, h * hd), lambda s, c, pt: (s, 0, 0)),
            scratch_shapes=[pltpu.VMEM((h, hd, PAGE_SIZE), F32), pltpu.VMEM((h, 1), F32), pltpu.VMEM((h, 1), F32),
                            pltpu.VMEM((h, hd, PAGE_SIZE), F32), pltpu.VMEM((h, 1), F32)]),
        out_shape=jax.ShapeDtypeStruct((s_n, 1, h * hd), F32),
        compiler_params=_params("parallel", "arbitrary"),
        name="fox_sample",
    )(page_table.reshape(-1), q, jnp.swapaxes(q, 1, 2), k_new, jnp.swapaxes(v_new, 1, 2),
      lf_new.reshape(s_n, h, 1), *([cache_kt] * pps), *([cache_vt] * pps), *([cache_lft] * pps))
    return out.reshape(s_n, h * hd)


def _moba_sample_kernel(pt_ref, q_ref, qt_ref, kn_ref, vnt_ref, *rest):
    pps = PAGES_PER_STEP
    k_refs, v_refs = rest[:pps], rest[pps:2 * pps]
    o_ref, qb_scr, s_scr, p_scr, l_scr, acc_scr = rest[2 * pps:]
    s = pl.program_id(0)
    c = pl.program_id(1)
    n_seq = pl.num_programs(0) - 1
    nc = s_scr.shape[0]
    nk = s_scr.shape[2]
    page = acc_scr.shape[2]
    has_keys = s < n_seq
    has_values = s >= 1

    @pl.when(has_keys & (c == 0))
    def _():
        for head in range(qb_scr.shape[0]):
            qb_scr[head] = _lane_bcast(qt_ref[:, head:head + 1], page)

    @pl.when(has_keys)
    def _():
        s_scr[c] = _page_scores(qb_scr, k_refs)

    @pl.when(has_values)
    def _():
        _accumulate_values(acc_scr, v_refs, p_scr[c], None)

    @pl.when(has_values & (c == nc - 1))
    def _():
        o_ref[...] = _finish_rows(acc_scr, l_scr[...])

    @pl.when(has_keys & (c == nc - 1))
    def _():
        bpc = nk // MOBA_BLOCK
        gates = []
        for cc in range(nc):
            sc = s_scr[cc]
            for b in range(bpc):
                seg = sc[:, b * MOBA_BLOCK:(b + 1) * MOBA_BLOCK]
                gates.append(jnp.broadcast_to(jnp.sum(seg, axis=1, keepdims=True) * (1.0 / MOBA_BLOCK),
                                              seg.shape))
        gate = jnp.concatenate(gates, axis=1)
        blk = (lax.broadcasted_iota(jnp.int32, gate.shape, 1) // MOBA_BLOCK).astype(F32)
        n_blk = float(gate.shape[1] // MOBA_BLOCK)
        sel = jnp.zeros(gate.shape, jnp.bool_)
        for _ in range(MOBA_TOPK):
            best = jnp.max(gate, axis=1, keepdims=True)
            pick = jnp.min(jnp.where(gate == best, blk, n_blk), axis=1, keepdims=True)
            chosen = blk == pick
            sel = sel | chosen
            gate = jnp.where(chosen, -jnp.inf, gate)
        s_self = jnp.sum(q_ref[...] * kn_ref[...], axis=1, keepdims=True) * SCALE
        s_all = jnp.concatenate([s_scr[cc] for cc in range(nc)], axis=1) * SCALE
        s_all = jnp.where(sel, s_all, NEG_INF)
        m = jnp.maximum(jnp.max(s_all, axis=1, keepdims=True), s_self)
        p = jnp.exp(s_all - m)
        p_self = jnp.exp(s_self - m)
        l_scr[...] = jnp.sum(p, axis=1, keepdims=True) + p_self
        _new_token_values(vnt_ref, acc_scr, p_self)
        for cc in range(nc):
            p_scr[cc] = p[:, cc * nk:(cc + 1) * nk]


def _moba_sample(page_table, q, k_new, v_new, cache_kt, cache_vt):
    s_n, h, hd = q.shape
    n_pages = page_table.shape[1]
    pps = PAGES_PER_STEP
    assert n_pages % pps == 0
    nc = n_pages // pps
    nk = pps * PAGE_SIZE
    assert nk % MOBA_BLOCK == 0 and n_pages * PAGE_SIZE // MOBA_BLOCK >= MOBA_TOPK

    last = s_n - 1

    def kpage(i):
        return lambda s, c, pt: (pt[jnp.minimum(s, last) * n_pages + c * pps + i],) + (0,) * 3

    def vpage(i):
        return lambda s, c, pt: (pt[jnp.maximum(s - 1, 0) * n_pages + c * pps + i],) + (0,) * 3

    cur = lambda shape: pl.BlockSpec((None,) + shape, lambda s, c, pt: (jnp.minimum(s, last), 0, 0))
    in_specs = [cur((h, hd)), cur((hd, h)), cur((h, hd)), cur((hd, h))]
    in_specs += [pl.BlockSpec((None, h, hd, PAGE_SIZE), kpage(i)) for i in range(pps)]
    in_specs += [pl.BlockSpec((None, h, hd, PAGE_SIZE), vpage(i)) for i in range(pps)]
    out = pl.pallas_call(
        _moba_sample_kernel,
        grid_spec=pltpu.PrefetchScalarGridSpec(
            num_scalar_prefetch=1, grid=(s_n + 1, nc), in_specs=in_specs,
            out_specs=pl.BlockSpec((None, 1, h * hd), lambda s, c, pt: (jnp.maximum(s - 1, 0), 0, 0)),
            scratch_shapes=[pltpu.VMEM((h, hd, PAGE_SIZE), F32), pltpu.VMEM((nc, h, nk), F32),
                            pltpu.VMEM((nc, h, nk), F32), pltpu.VMEM((h, 1), F32),
                            pltpu.VMEM((h, hd, PAGE_SIZE), F32)]),
        out_shape=jax.ShapeDtypeStruct((s_n, 1, h * hd), F32),
        compiler_params=_params("arbitrary", "arbitrary"),
        name="moba_sample",
    )(page_table.reshape(-1), q, jnp.swapaxes(q, 1, 2), k_new, jnp.swapaxes(v_new, 1, 2),
      *([cache_kt] * pps), *([cache_vt] * pps))
    return out.reshape(s_n, h * hd)


def _merge_kernel(h_ref, oa_ref, ob_ref, gl_ref, ga_ref, woa_ref, wob_ref, wout_ref, o_ref):
    d = h_ref.shape[1]
    ya = _dot(oa_ref[...].astype(BF16), woa_ref[...])
    yb = _dot(ob_ref[...].astype(BF16), wob_ref[...])
    gl = gl_ref[...]
    merged = jax.nn.sigmoid(gl[:, :d]) * ya + jax.nn.sigmoid(gl[:, d:]) * yb
    o_ref[...] = h_ref[...] + ga_ref[...] * _dot(merged.astype(BF16), wout_ref[...])


def _merge(h, oa, ob, gl, mod, woa, wob, wout, tokens_per_seq):
    t, d = h.shape
    tm = min(t, 512)
    tps = max(tokens_per_seq // tm, 1)
    tok = lambda i: (i, 0)
    const = lambda i: (0, 0)
    return pl.pallas_call(
        _merge_kernel,
        grid=(t // tm,),
        in_specs=[pl.BlockSpec((tm, d), tok), pl.BlockSpec((tm, oa.shape[1]), tok),
                  pl.BlockSpec((tm, ob.shape[1]), tok), pl.BlockSpec((tm, gl.shape[1]), tok),
                  _mod_spec(mod, 5, tm, tps),
                  pl.BlockSpec(woa.shape, const), pl.BlockSpec(wob.shape, const), pl.BlockSpec(wout.shape, const)],
        out_specs=pl.BlockSpec((tm, d), tok),
        out_shape=jax.ShapeDtypeStruct((t, d), F32),
        compiler_params=_params("parallel"),
        name="merge",
    )(h, oa, ob, gl, mod, woa, wob, wout)


def _rope_tables(pos):
    half = HEAD_DIM // 2
    inv = ROPE_THETA ** (-jnp.arange(half, dtype=F32) / half)
    ang = pos.astype(F32)[:, None] * inv[None, :]
    cos, sin = jnp.cos(ang), jnp.sin(ang)
    reps = LANES // HEAD_DIM
    return (jnp.tile(jnp.concatenate([cos, cos], axis=1), (1, reps)),
            jnp.tile(jnp.concatenate([-sin, sin], axis=1), (1, reps)))


def kernel(x_prompt, x_sample, cache_moba_k, cache_moba_v, cache_fox_k, cache_fox_v, cache_fox_logf, page_table,
           c_prompt, c_sample, w_ada, b_ada, g_ff1, w_ff1_in, w_ff1_out, g_mix, w_mix, b_forget, w_o_moba,
           w_o_fox, w_out, g_ff2, w_ff2_in, w_ff2_out, g_final):
    bsz, l, d = x_prompt.shape
    s_n, dec_seq, _ = x_sample.shape
    assert dec_seq == 1
    depth = w_ada.shape[0]
    h_moba, h_fox = cache_moba_k.shape[3], cache_fox_k.shape[3]
    w_moba, w_fox = h_moba * HEAD_DIM, h_fox * HEAD_DIM
    past_len = page_table.shape[1] * PAGE_SIZE
    n_qkv = 3 * w_moba + 3 * w_fox
    n_blocks = l // MOBA_BLOCK
    assert l % MOBA_BLOCK == 0 and MOBA_LANE0 + n_blocks <= LANES

    cos_p, sin_p = _rope_tables(jnp.arange(l, dtype=jnp.int32))
    cos_s, sin_s = _rope_tables(jnp.full((1,), past_len, jnp.int32))
    pad_rows = (-(s_n + bsz)) % 8
    c_all = jnp.concatenate([c_sample, c_prompt, jnp.zeros((pad_rows, d), F32)], axis=0)
    block_lanes = (jnp.arange(LANES, dtype=jnp.int32)[None, :] - MOBA_LANE0
                   == jnp.arange(l, dtype=jnp.int32)[:, None] // MOBA_BLOCK).astype(BF16)

    hp = x_prompt.reshape(bsz * l, d)
    hs = x_sample.reshape(s_n, d)
    st_p = [[] for _ in range(5)]
    st_s = [[] for _ in range(5)]
    for layer in range(depth):
        mod = _modulation(c_all, w_ada[layer], b_ada[layer])
        mod_s = mod[:, :s_n]
        mod_p = mod[:, s_n:s_n + bsz].reshape(N_MOD, bsz, 1, d)
        w1i, w1o = w_ff1_in[layer].astype(BF16), w_ff1_out[layer].astype(BF16)
        w2i, w2o = w_ff2_in[layer].astype(BF16), w_ff2_out[layer].astype(BF16)
        wm = w_mix[layer]
        wqkv = wm[:, :n_qkv].astype(BF16)
        wf = jnp.pad(wm[:, n_qkv:n_qkv + h_fox], ((0, 0), (0, LANES - h_fox))).astype(BF16)
        wg = wm[:, n_qkv + h_fox:].astype(BF16)
        bfor = jnp.pad(b_forget[layer], (0, LANES - h_fox)).reshape(1, LANES)
        woa, wob, wout = w_o_moba[layer].astype(BF16), w_o_fox[layer].astype(BF16), w_out[layer].astype(BF16)
        last = layer == depth - 1
        mix_kw = dict(w_moba=w_moba, w_fox=w_fox, n_fox=h_fox)

        hp = _ffn(hp, mod_p, 0, g_ff1[layer], w1i, w1o, l)
        (kat, vat, kbt, vbt, lft, gl, ka16, kb16, qat16, vat16, qbt16, vbt16, kbias, cumt, km, kstat) = _mix(
            hp, mod_p, g_mix[layer], cos_p, sin_p, wqkv, wf, bfor, wg, seq_shape=(bsz, l), **mix_kw)
        km = jnp.pad(km.reshape(bsz, n_blocks, w_moba), ((0, 0), (MOBA_LANE0, LANES - MOBA_LANE0 - n_blocks), (0, 0)))
        oa = _prompt_attn(qat16, ka16.reshape(bsz, l, w_moba), vat16, block_lanes, km, moba=True, n_fox=h_fox)
        ob = _prompt_attn(qbt16, kb16.reshape(bsz, l, w_fox), vbt16, kbias.reshape(bsz, l, LANES), cumt, kstat,
                          moba=False, n_fox=h_fox)
        hp = _merge(hp, oa.reshape(bsz * l, w_moba), ob.reshape(bsz * l, w_fox), gl, mod_p, woa, wob, wout, l)
        hp = _ffn(hp, mod_p, 6, g_ff2[layer], w2i, w2o, l, g_final=g_final if last else None)
        heads = lambda a, h: jnp.transpose(a.reshape(bsz, h, HEAD_DIM, l), (0, 3, 1, 2))
        for i, a in enumerate((heads(kat, h_moba), heads(vat, h_moba), heads(kbt, h_fox), heads(vbt, h_fox),
                               jnp.swapaxes(lft, 1, 2))):
            st_p[i].append(a)

        hs = _ffn(hs, mod_s, 0, g_ff1[layer], w1i, w1o, 1)
        ka, va, kb, vb, lf, gl, qa, qb = _mix(hs, mod_s, g_mix[layer], cos_s, sin_s, wqkv, wf, bfor, wg,
                                              seq_shape=None, **mix_kw)
        ka, va, qa = (a.reshape(s_n, h_moba, HEAD_DIM) for a in (ka, va, qa))
        kb, vb, qb = (a.reshape(s_n, h_fox, HEAD_DIM) for a in (kb, vb, qb))
        pages_t = lambda a: jnp.transpose(a[layer], (0, 2, 3, 1))
        oa = _moba_sample(page_table, qa, ka, va, pages_t(cache_moba_k), pages_t(cache_moba_v))
        ob = _fox_sample(page_table, qb, kb, vb, lf, pages_t(cache_fox_k), pages_t(cache_fox_v),
                         jnp.swapaxes(cache_fox_logf[layer], 1, 2))
        hs = _merge(hs, oa, ob, gl, mod_s, woa, wob, wout, 1)
        hs = _ffn(hs, mod_s, 6, g_ff2[layer], w2i, w2o, 1, g_final=g_final if last else None)
        for i, a in enumerate((ka.reshape(s_n, 1, h_moba, HEAD_DIM), va.reshape(s_n, 1, h_moba, HEAD_DIM),
                               kb.reshape(s_n, 1, h_fox, HEAD_DIM), vb.reshape(s_n, 1, h_fox, HEAD_DIM),
                               lf.reshape(s_n, 1, h_fox))):
            st_s[i].append(a)

    return (hp.reshape(bsz, l, d), hs.reshape(s_n, 1, d),
            *(jnp.stack(a) for a in st_p), *(jnp.stack(a) for a in st_s))
```
